```python
import math
import jax, jax.numpy as jnp
from jax import lax
import numpy as np

D_MODEL = 1024
BATCH = 2
SEQ = 8192
DEPTH = 1

DN_HEADS = 8
DN_HEAD_DIM = 128
DN_WIDTH = DN_HEADS * DN_HEAD_DIM
DN_CONV = 5
CHUNK = 64
SC_WIDTH = D_MODEL
SC_CONV = 3
N_BRANCH = 2
IN_SIZES = (3 * DN_WIDTH, DN_WIDTH, 2 * DN_HEADS, 2 * DN_HEADS, 3 * SC_WIDTH, N_BRANCH * D_MODEL)
IN_WIDTH = sum(IN_SIZES)
N_EXPERTS = 32
TOP_K = 4
D_FF = D_MODEL
SWIGLU_LIMIT = 7.0
SWIGLU_ALPHA = 1.702
MOE_BLOCK = 256
EPS = 1e-6

kernel_name = "hybrid_gdn_shortconv_moe_adaln_encoder"


def rmsnorm(x, w):
    xf = x.astype(jnp.float32)
    xf = xf * lax.rsqrt(jnp.mean(xf * xf, axis=-1, keepdims=True) + EPS)
    return (xf * w.astype(jnp.float32)).astype(x.dtype)


def l2norm(x):
    return x * lax.rsqrt(jnp.sum(x * x, axis=-1, keepdims=True) + EPS)


def dwconv_centred(x, w):
    width, ch = w.shape
    pad = width // 2
    return lax.conv_general_dilated(
        x, w[:, None, :].astype(x.dtype), window_strides=(1,), padding=[(pad, pad)],
        dimension_numbers=("NWC", "WIO", "NWC"), feature_group_count=ch)


def unit_lower_inverse(L):
    c = L.shape[-1]
    n = -L
    p = jnp.eye(c, dtype=L.dtype) + n
    npow = n
    for _ in range(int(math.log2(c)) - 1):
        npow = npow @ npow
        p = p + p @ npow
    return p


def gated_delta_rule(q, k, v, beta, g):
    bsz, nh, t, dk = q.shape
    dv = v.shape[-1]
    n = t // CHUNK
    q = q * (dk ** -0.5)
    ch = lambda a: a.reshape(bsz, nh, n, CHUNK, *a.shape[3:])
    q, k, v, beta, g = ch(q), ch(k), ch(v), ch(beta), ch(g)
    g = jnp.cumsum(g, axis=-1)
    idx = jnp.arange(CHUNK)
    incl = idx[:, None] >= idx[None, :]
    strict = idx[:, None] > idx[None, :]
    decay = jnp.exp(jnp.where(incl, g[..., :, None] - g[..., None, :], -jnp.inf))
    k_beta = k * beta[..., None]
    L = jnp.where(strict, jnp.einsum("bhnid,bhnjd->bhnij", k_beta, k) * decay, 0.0)
    t_inv = unit_lower_inverse(L)
    u = t_inv @ (v * beta[..., None])
    w = t_inv @ (k_beta * jnp.exp(g)[..., None])
    a_qk = jnp.einsum("bhnid,bhnjd->bhnij", q, k) * decay
    g_last = g[..., -1]
    k_dec = k * jnp.exp(g_last[..., None] - g)[..., None]
    q_dec = q * jnp.exp(g)[..., None]

    def step(state, xs):
        q_c, k_c, u_c, w_c, a_c, gl = xs
        v_new = u_c - w_c @ state
        o_c = q_c @ state + a_c @ v_new
        state = state * jnp.exp(gl)[..., None, None] + jnp.einsum("bhck,bhcv->bhkv", k_c, v_new)
        return state, o_c

    xs = tuple(jnp.moveaxis(a, 2, 0) for a in (q_dec, k_dec, u, w, a_qk, g_last))
    s0 = jnp.zeros((bsz, nh, dk, dv), jnp.float32)
    _, o = lax.scan(step, s0, xs)
    return jnp.moveaxis(o, 0, 2).reshape(bsz, nh, t, dv)


def hybrid_mixer(h, w_in, conv_qkv_w, a_log, dt_bias, onorm_w, w_up_a, conv_sc_w, w_out_sc, w_o):
    bsz, t, _ = h.shape
    p = h @ w_in
    cuts = np.cumsum(IN_SIZES)[:-1].tolist()
    qkv, z, b_raw, a_raw, sc, gates = jnp.split(p, cuts, axis=-1)

    qkv = jax.nn.silu(dwconv_centred(qkv, conv_qkv_w)).astype(jnp.float32)
    heads = lambda a: a.reshape(bsz, t, DN_HEADS, DN_HEAD_DIM).transpose(0, 2, 1, 3)
    q, k, v = (heads(a) for a in jnp.split(qkv, 3, axis=-1))
    q, k = l2norm(q), l2norm(k)
    beta = jax.nn.sigmoid(b_raw.astype(jnp.float32)).reshape(bsz, t, 2, DN_HEADS)
    g = -jnp.exp(a_log.astype(jnp.float32)) * jax.nn.softplus(
        a_raw.astype(jnp.float32).reshape(bsz, t, 2, DN_HEADS) + dt_bias.astype(jnp.float32))
    bh = lambda a, d: a[:, :, d].transpose(0, 2, 1)
    o_fwd = gated_delta_rule(q, k, v, bh(beta, 0), bh(g, 0))
    o_bwd = jnp.flip(gated_delta_rule(jnp.flip(q, 2), jnp.flip(k, 2), jnp.flip(v, 2),
                                      jnp.flip(bh(beta, 1), -1), jnp.flip(bh(g, 1), -1)), 2)
    o = (o_fwd + o_bwd).transpose(0, 2, 1, 3)
    o = o * lax.rsqrt(jnp.mean(o * o, axis=-1, keepdims=True) + EPS) * onorm_w.astype(jnp.float32)
    o = o.astype(h.dtype) * jax.nn.silu(z.reshape(bsz, t, DN_HEADS, DN_HEAD_DIM))
    y_a = o.reshape(bsz, t, DN_WIDTH) @ w_up_a

    b_gate, c_gate, u_in = jnp.split(sc, 3, axis=-1)
    y_b = (b_gate * dwconv_centred(c_gate * u_in, conv_sc_w)) @ w_out_sc

    g_a, g_b = jnp.split(gates, N_BRANCH, axis=-1)
    return (jax.nn.sigmoid(g_a) * y_a + jax.nn.sigmoid(g_b) * y_b) @ w_o


def moe_ffn(h, router_w, router_b, w1, b1, w2, b2):
    bsz, t, d = h.shape
    n_tok = bsz * t
    hf = h.reshape(n_tok, d)
    logits = (hf @ router_w + router_b).astype(jnp.float32)
    top_val, top_idx = lax.top_k(logits, TOP_K)
    gates = jax.nn.softmax(top_val, axis=-1)
    m = n_tok * TOP_K
    e_flat = top_idx.reshape(m)
    t_flat = jnp.repeat(jnp.arange(n_tok, dtype=jnp.int32), TOP_K)
    g_flat = gates.reshape(m)
    order = jnp.argsort(e_flat)
    e_s, t_s, g_s = e_flat[order], t_flat[order], g_flat[order]
    counts = jnp.bincount(e_flat, length=N_EXPERTS)
    padded = (counts + MOE_BLOCK - 1) // MOE_BLOCK * MOE_BLOCK
    start = jnp.cumsum(counts) - counts
    pad_end = jnp.cumsum(padded)
    pad_start = pad_end - padded
    dest = pad_start[e_s] + (jnp.arange(m) - start[e_s])
    n_blocks = -(-m // MOE_BLOCK) + N_EXPERTS
    m_pad = n_blocks * MOE_BLOCK
    tok_buf = jnp.full((m_pad,), n_tok, jnp.int32).at[dest].set(t_s)
    gate_buf = jnp.zeros((m_pad,), jnp.float32).at[dest].set(g_s)
    block_exp = jnp.minimum(
        jnp.searchsorted(pad_end, jnp.arange(n_blocks) * MOE_BLOCK, side="right"), N_EXPERTS - 1)
    h_pad = jnp.concatenate([hf, jnp.zeros((1, d), hf.dtype)], axis=0)
    xb = h_pad[tok_buf].reshape(n_blocks, MOE_BLOCK, d)

    def expert_block(args):
        x_blk, e = args
        gu = x_blk @ w1[e] + b1[e]
        gate, up = jnp.split(gu, 2, axis=-1)
        gate = jnp.minimum(gate, SWIGLU_LIMIT)
        up = jnp.clip(up, -SWIGLU_LIMIT, SWIGLU_LIMIT)
        act = gate * jax.nn.sigmoid(SWIGLU_ALPHA * gate) * (up + 1.0)
        return act @ w2[e] + b2[e]

    yb = lax.map(expert_block, (xb, block_exp)).reshape(m_pad, d)
    yb = yb * gate_buf[:, None].astype(yb.dtype)
    y = jax.ops.segment_sum(yb, tok_buf, num_segments=n_tok + 1)[:n_tok]
    return y.reshape(bsz, t, d)


def setup_inputs(seed: int = 0) -> dict:
    key = jax.random.key(seed)
    ks = jax.random.split(key, 24)
    nrm = lambda k, shape, s: jax.random.normal(k, shape, jnp.float32) * s
    dt = jnp.exp(jax.random.uniform(ks[6], (DEPTH, 2, DN_HEADS), jnp.float32,
                                    math.log(1e-3), math.log(1e-1)))
    return {
        "x": nrm(ks[0], (BATCH, SEQ, D_MODEL), 1.0),
        "c": nrm(ks[1], (BATCH, D_MODEL), 1.0),
        "ada_w": nrm(ks[2], (DEPTH, D_MODEL, 6 * D_MODEL), 0.5 * D_MODEL ** -0.5),
        "ada_b": nrm(ks[3], (DEPTH, 6 * D_MODEL), 0.02),
        "norm1_w": 1.0 + nrm(ks[4], (DEPTH, D_MODEL), 0.02),
        "w_in": nrm(ks[5], (DEPTH, D_MODEL, IN_WIDTH), D_MODEL ** -0.5),
        "conv_qkv_w": nrm(ks[7], (DEPTH, DN_CONV, 3 * DN_WIDTH), DN_CONV ** -0.5),
        "a_log": jnp.log(jax.random.uniform(ks[8], (DEPTH, 2, DN_HEADS), jnp.float32, 1.0, 16.0)),
        "dt_bias": dt + jnp.log(-jnp.expm1(-dt)),
        "onorm_w": 1.0 + nrm(ks[9], (DEPTH, DN_HEAD_DIM), 0.02),
        "w_up_a": nrm(ks[10], (DEPTH, DN_WIDTH, D_MODEL), DN_WIDTH ** -0.5),
        "conv_sc_w": nrm(ks[11], (DEPTH, SC_CONV, SC_WIDTH), SC_CONV ** -0.5),
        "w_out_sc": nrm(ks[12], (DEPTH, SC_WIDTH, D_MODEL), SC_WIDTH ** -0.5),
        "w_o": nrm(ks[13], (DEPTH, D_MODEL, D_MODEL), D_MODEL ** -0.5),
        "norm2_w": 1.0 + nrm(ks[14], (DEPTH, D_MODEL), 0.02),
        "router_w": nrm(ks[15], (DEPTH, D_MODEL, N_EXPERTS), D_MODEL ** -0.5),
        "router_b": nrm(ks[16], (DEPTH, N_EXPERTS), 0.01),
        "moe_w1": nrm(ks[17], (DEPTH, N_EXPERTS, D_MODEL, 2 * D_FF), D_MODEL ** -0.5),
        "moe_b1": nrm(ks[18], (DEPTH, N_EXPERTS, 2 * D_FF), 0.01),
        "moe_w2": nrm(ks[19], (DEPTH, N_EXPERTS, D_FF, D_MODEL), D_FF ** -0.5),
        "moe_b2": nrm(ks[20], (DEPTH, N_EXPERTS, D_MODEL), 0.01),
        "final_norm_w": 1.0 + nrm(ks[21], (D_MODEL,), 0.02),
    }


def reference(x, c, ada_w, ada_b, norm1_w, w_in, conv_qkv_w, a_log, dt_bias, onorm_w, w_up_a,
              conv_sc_w, w_out_sc, w_o, norm2_w, router_w, router_b, moe_w1, moe_b1, moe_w2,
              moe_b2, final_norm_w):
    c_act = jax.nn.silu(c)
    for l in range(DEPTH):
        mod = c_act @ ada_w[l] + ada_b[l]
        sh_m, sc_m, gt_m, sh_f, sc_f, gt_f = (m[:, None, :] for m in jnp.split(mod, 6, axis=-1))
        h = rmsnorm(x, norm1_w[l]) * (1.0 + sc_m) + sh_m
        x = x + gt_m * hybrid_mixer(h, w_in[l], conv_qkv_w[l], a_log[l], dt_bias[l], onorm_w[l],
                                    w_up_a[l], conv_sc_w[l], w_out_sc[l], w_o[l])
        h = rmsnorm(x, norm2_w[l]) * (1.0 + sc_f) + sh_f
        x = x + gt_f * moe_ffn(h, router_w[l], router_b[l], moe_w1[l], moe_b1[l], moe_w2[l], moe_b2[l])
    return rmsnorm(x, final_norm_w)
```

```python
import functools

import jax
import jax.numpy as jnp
from jax import lax
from jax.experimental import pallas as pl
from jax.experimental.pallas import tpu as pltpu

F32 = jnp.float32
BF16 = jnp.bfloat16

EPS = 1e-6
N_HEADS = 8
HEAD_DIM = 128
CHUNK = 64
N_EXPERTS = 32
TOP_K = 4
SWIGLU_LIMIT = 7.0
SWIGLU_ALPHA = 1.702
LANES = 128
SUBLANES = 8
MOE_ROWS = 256
VMEM_LIMIT = 56 * 1024 * 1024


def _sigmoid(x):
    return 1.0 / (1.0 + jnp.exp(-x))


def _softplus(x):
    return jnp.maximum(x, 0.0) + jnp.log(1.0 + jnp.exp(-jnp.abs(x)))


def _dot(a, b):
    return jnp.dot(a, b, preferred_element_type=F32)


def _params(*sem):
    return pltpu.CompilerParams(dimension_semantics=sem, vmem_limit_bytes=VMEM_LIMIT)


def _ada_body(c_ref, w_ref, b_ref, o_ref):
    c = c_ref[...]
    ca = c * _sigmoid(c)
    o_ref[...] = _dot(ca.astype(BF16), w_ref[...].astype(BF16)) + b_ref[...]


def _ada_mod(c, ada_w, ada_b):
    bsz, d = c.shape
    n_out = ada_w.shape[1]
    cp = jnp.pad(c, ((0, SUBLANES - bsz), (0, 0)))
    out = pl.pallas_call(
        _ada_body,
        grid=(n_out // d,),
        in_specs=[pl.BlockSpec((SUBLANES, d), lambda j: (0, 0)),
                  pl.BlockSpec((d, d), lambda j: (0, j)),
                  pl.BlockSpec((1, d), lambda j: (0, j))],
        out_specs=pl.BlockSpec((SUBLANES, d), lambda j: (0, j)),
        out_shape=jax.ShapeDtypeStruct((SUBLANES, n_out), F32),
        compiler_params=_params("parallel"),
        name="ada_mod",
    )(cp, ada_w, ada_b.reshape(1, n_out))
    return out[:bsz]


def _inproj_body(x_ref, nw_ref, sc_ref, sh_ref, w_ref, ws_ref, o_ref, os_ref, h_scr):
    @pl.when(pl.program_id(1) == 0)
    def _():
        x = x_ref[...]
        ms = jnp.mean(x * x, axis=-1, keepdims=True)
        xn = x * lax.rsqrt(ms + EPS) * nw_ref[...]
        h = (xn * (1.0 + sc_ref[0]) + sh_ref[0]).astype(BF16)
        h_scr[...] = h
        os_ref[...] = _dot(h, ws_ref[...])

    o_ref[...] = _dot(h_scr[...], w_ref[...])


def _in_projection(x2d, norm_w, sc_m, sh_m, w_main, w_small, seq):
    n, d = x2d.shape
    width = w_main.shape[1]
    tm = min(1024, seq)
    tn = 1024
    tiles_per_batch = seq // tm
    return pl.pallas_call(
        _inproj_body,
        grid=(n // tm, width // tn),
        in_specs=[pl.BlockSpec((tm, d), lambda i, j: (i, 0)),
                  pl.BlockSpec((1, d), lambda i, j: (0, 0)),
                  pl.BlockSpec((1, 1, d), lambda i, j: (i // tiles_per_batch, 0, 0)),
                  pl.BlockSpec((1, 1, d), lambda i, j: (i // tiles_per_batch, 0, 0)),
                  pl.BlockSpec((d, tn), lambda i, j: (0, j)),
                  pl.BlockSpec((d, LANES), lambda i, j: (0, 0))],
        out_specs=[pl.BlockSpec((tm, tn), lambda i, j: (i, j)),
                   pl.BlockSpec((tm, LANES), lambda i, j: (i, 0))],
        out_shape=[jax.ShapeDtypeStruct((n, width), F32),
                   jax.ShapeDtypeStruct((n, LANES), F32)],
        scratch_shapes=[pltpu.VMEM((tm, d), BF16)],
        compiler_params=_params("parallel", "arbitrary"),
        name="in_proj",
    )(x2d, norm_w.reshape(1, d), sc_m, sh_m, w_main, w_small)


def _conv_rows(ext, cw, width, rows):
    length = rows + 2 * SUBLANES
    pad = width // 2
    acc = ext * cw[pad:pad + 1]
    for w in range(width):
        if w != pad:
            acc = acc + pltpu.roll(ext, (pad - w) % length, axis=0) * cw[w:w + 1]
    return acc[SUBLANES:SUBLANES + rows]


def _qkv_conv_body(cur_ref, prev_ref, next_ref, cw_ref, o_ref, *, tiles_per_batch, width,
                   n_norm_blocks):
    t_in_b = pl.program_id(0) % tiles_per_batch
    rows = cur_ref.shape[0]
    prev = jnp.where(t_in_b == 0, 0.0, prev_ref[...])
    nxt = jnp.where(t_in_b == tiles_per_batch - 1, 0.0, next_ref[...])
    ext = jnp.concatenate([prev, cur_ref[...], nxt], axis=0)
    y = _conv_rows(ext, cw_ref[...], width, rows)
    y = y * _sigmoid(y)
    normed = pl.program_id(1) < n_norm_blocks
    for h in range(y.shape[1] // HEAD_DIM):
        blk = y[:, h * HEAD_DIM:(h + 1) * HEAD_DIM]
        ss = jnp.sum(blk * blk, axis=-1, keepdims=True)
        scale = jnp.where(normed, lax.rsqrt(ss + EPS), 1.0)
        o_ref[:, h * HEAD_DIM:(h + 1) * HEAD_DIM] = blk * scale


def _halo_specs(rows, cols, col_block, n_rows):
    per = rows // SUBLANES
    last = n_rows // SUBLANES - 1
    cur = pl.BlockSpec((rows, cols), lambda i, j: (i, col_block(j)))
    prev = pl.BlockSpec((SUBLANES, cols), lambda i, j: (jnp.maximum(i * per - 1, 0), col_block(j)))
    nxt = pl.BlockSpec((SUBLANES, cols), lambda i, j: (jnp.minimum((i + 1) * per, last), col_block(j)))
    return cur, prev, nxt


def _qkv_conv(p_main, conv_w, seq):
    n = p_main.shape[0]
    width, ch = conv_w.shape
    cols = N_HEADS * HEAD_DIM
    rows = min(512, seq)
    cw = jnp.pad(conv_w, ((0, SUBLANES - width), (0, 0)))
    cur, prev, nxt = _halo_specs(rows, cols, lambda j: j, n)
    body = functools.partial(_qkv_conv_body, tiles_per_batch=seq // rows, width=width,
                             n_norm_blocks=2)
    return pl.pallas_call(
        body,
        grid=(n // rows, ch // cols),
        in_specs=[cur, prev, nxt, pl.BlockSpec((SUBLANES, cols), lambda i, j: (0, j))],
        out_specs=pl.BlockSpec((rows, cols), lambda i, j: (i, j)),
        out_shape=jax.ShapeDtypeStruct((n, ch), F32),
        compiler_params=_params("parallel", "parallel"),
        name="qkv_conv",
    )(p_main, p_main, p_main, cw)


def _sc_conv_body(b_ref, c_ref, cp_ref, cn_ref, u_ref, up_ref, un_ref, cw_ref, o_ref, *,
                  tiles_per_batch, width):
    t_in_b = pl.program_id(0) % tiles_per_batch
    rows = c_ref.shape[0]
    prev = jnp.where(t_in_b == 0, 0.0, cp_ref[...] * up_ref[...])
    nxt = jnp.where(t_in_b == tiles_per_batch - 1, 0.0, cn_ref[...] * un_ref[...])
    ext = jnp.concatenate([prev, c_ref[...] * u_ref[...], nxt], axis=0)
    o_ref[...] = b_ref[...] * _conv_rows(ext, cw_ref[...], width, rows)


def _sc_conv(p_main, conv_w, seq, first_col_block):
    n = p_main.shape[0]
    width, ch = conv_w.shape
    rows = min(512, seq)
    cw = jnp.pad(conv_w, ((0, SUBLANES - width), (0, 0)))
    b_spec = pl.BlockSpec((rows, ch), lambda i, j: (i, first_col_block))
    c_specs = _halo_specs(rows, ch, lambda j: first_col_block + 1, n)
    u_specs = _halo_specs(rows, ch, lambda j: first_col_block + 2, n)
    body = functools.partial(_sc_conv_body, tiles_per_batch=seq // rows, width=width)
    return pl.pallas_call(
        body,
        grid=(n // rows, 1),
        in_specs=[b_spec, *c_specs, *u_specs, pl.BlockSpec((SUBLANES, ch), lambda i, j: (0, 0))],
        out_specs=pl.BlockSpec((rows, ch), lambda i, j: (i, 0)),
        out_shape=jax.ShapeDtypeStruct((n, ch), F32),
        compiler_params=_params("parallel", "arbitrary"),
        name="sc_conv",
    )(p_main, p_main, p_main, p_main, p_main, p_main, p_main, cw)


def _delta_local_body(q_ref, k_ref, v_ref, s_ref, alog_ref, dtb_ref,
                      uf_ref, ub_ref, wf_ref, wb_ref, qf_ref, qb_ref, a_ref, kt_ref, ef_ref, eb_ref,
                      beta_scr, g_scr, gt_scr):
    h = pl.program_id(1)
    rows = q_ref.shape[0]
    n_chunks = rows // CHUNK

    @pl.when(h == 0)
    def _():
        s = s_ref[...]
        beta_scr[...] = _sigmoid(s)
        g = -jnp.exp(alog_ref[...]) * _softplus(s + dtb_ref[...])
        pos = lax.broadcasted_iota(jnp.int32, (rows, LANES), 0) & (CHUNK - 1)
        gp = g
        gs = g
        step = 1
        while step < CHUNK:
            gp = gp + jnp.where(pos >= step, pltpu.roll(gp, step, axis=0), 0.0)
            gs = gs + jnp.where(pos < CHUNK - step, pltpu.roll(gs, rows - step, axis=0), 0.0)
            step *= 2
        lane = lax.broadcasted_iota(jnp.int32, (rows, LANES), 1)
        gc = jnp.where(lane >= 3 * N_HEADS, gs, gp)
        g_scr[...] = gc
        for c2 in range(rows // LANES):
            gt_scr[c2] = gc[c2 * LANES:(c2 + 1) * LANES, :].T

    lane = lax.broadcasted_iota(jnp.int32, (rows, LANES), 1)

    def pick(tile, idx):
        return jnp.sum(jnp.where(lane == idx, tile, 0.0), axis=-1, keepdims=True)

    beta_all = beta_scr[...]
    g_all = g_scr[...]
    beta_cols = (pick(beta_all, h), pick(beta_all, N_HEADS + h))
    g_cols = (pick(g_all, 2 * N_HEADS + h), pick(g_all, 3 * N_HEADS + h))

    ii = lax.broadcasted_iota(jnp.int32, (CHUNK, CHUNK), 0)
    jj = lax.broadcasted_iota(jnp.int32, (CHUNK, CHUNK), 1)
    eye = (ii == jj).astype(F32)
    incl = (ii >= jj, ii <= jj)
    strict = (ii > jj, ii < jj)
    u_refs = (uf_ref, ub_ref)
    w_refs = (wf_ref, wb_ref)
    qd_refs = (qf_ref, qb_ref)
    e_refs = (ef_ref, eb_ref)

    for c in range(n_chunks):
        cs = slice(c * CHUNK, (c + 1) * CHUNK)
        qc = q_ref[cs, :] * (HEAD_DIM ** -0.5)
        kc = k_ref[cs, :]
        vc = v_ref[cs, :]
        kcb = kc.astype(BF16)
        a_parts = []
        kd_parts = []
        for d in range(2):
            beta = beta_cols[d][cs]
            gcol = g_cols[d][cs]
            lo = (c % 2) * CHUNK
            grow = gt_scr[c // 2, pl.ds(2 * N_HEADS + d * N_HEADS + h, 1), :][:, lo:lo + CHUNK]
            decay = jnp.exp(jnp.where(incl[d], gcol - grow, -jnp.inf))
            kb = kc * beta
            lhs = jnp.concatenate([kb, qc], axis=0).astype(BF16)
            m1 = lax.dot_general(lhs, kcb, (((1,), (1,)), ((), ())), preferred_element_type=F32)
            nm = -jnp.where(strict[d], m1[:CHUNK] * decay, 0.0)
            a_parts.append(m1[CHUNK:] * decay)
            p = eye + nm
            nmb = nm.astype(BF16)
            npow = _dot(nmb, nmb)
            n_iter = CHUNK.bit_length() - 2
            for it in range(n_iter):
                npb = npow.astype(BF16)
                if it + 1 < n_iter:
                    x = _dot(jnp.concatenate([p, npow], axis=0).astype(BF16), npb)
                    p = p + x[:CHUNK]
                    npow = x[CHUNK:]
                else:
                    p = p + _dot(p.astype(BF16), npb)
            eg = jnp.exp(gcol)
            rhs = jnp.concatenate([vc * beta, kb * eg], axis=1).astype(BF16)
            uw = _dot(p.astype(BF16), rhs)
            g_last = gcol[CHUNK - 1:CHUNK] if d == 0 else gcol[0:1]
            u_refs[d][cs, :] = uw[:, :HEAD_DIM]
            w_refs[d][cs, :] = uw[:, HEAD_DIM:].astype(BF16)
            qd_refs[d][cs, :] = (qc * eg).astype(BF16)
            kd_parts.append(kc * jnp.exp(g_last - gcol))
            e_refs[d][c * SUBLANES:(c + 1) * SUBLANES, :] = jnp.broadcast_to(
                jnp.exp(g_last), (SUBLANES, LANES))
        a_ref[cs, :] = jnp.concatenate(a_parts, axis=1).astype(BF16)
        kt_ref[c * 2 * CHUNK:(c + 1) * 2 * CHUNK, :] = (
            jnp.concatenate(kd_parts, axis=0).T.astype(BF16))


def _delta_local(qkv, p_small, a_log, dt_bias, seq):
    n = qkv.shape[0]
    cols = N_HEADS * HEAD_DIM
    rows = min(512, seq)
    alog_row = jnp.pad(a_log.reshape(1, -1), ((0, 0), (2 * N_HEADS, LANES - 4 * N_HEADS)))
    dtb_row = jnp.pad(dt_bias.reshape(1, -1), ((0, 0), (2 * N_HEADS, LANES - 4 * N_HEADS)))
    blk = lambda off: pl.BlockSpec((rows, HEAD_DIM), lambda i, h: (i, off + h))
    row_spec = pl.BlockSpec((1, LANES), lambda i, h: (0, 0))
    out_blk = pl.BlockSpec((rows, HEAD_DIM), lambda i, h: (i, h))
    f32_out = jax.ShapeDtypeStruct((n, cols), F32)
    bf_out = jax.ShapeDtypeStruct((n, cols), BF16)
    eg_out = jax.ShapeDtypeStruct((n // SUBLANES, cols), F32)
    eg_blk = pl.BlockSpec((rows // SUBLANES, HEAD_DIM), lambda i, h: (i, h))
    return pl.pallas_call(
        _delta_local_body,
        grid=(n // rows, N_HEADS),
        in_specs=[blk(0), blk(N_HEADS), blk(2 * N_HEADS),
                  pl.BlockSpec((rows, LANES), lambda i, h: (i, 0)), row_spec, row_spec],
        out_specs=[out_blk, out_blk, out_blk, out_blk, out_blk, out_blk, out_blk,
                   pl.BlockSpec((2 * rows, HEAD_DIM), lambda i, h: (i, h)), eg_blk, eg_blk],
        out_shape=[f32_out, f32_out, bf_out, bf_out, bf_out, bf_out, bf_out,
                   jax.ShapeDtypeStruct((2 * n, cols), BF16), eg_out, eg_out],
        scratch_shapes=[pltpu.VMEM((rows, LANES), F32), pltpu.VMEM((rows, LANES), F32),
                        pltpu.VMEM((rows // LANES, LANES, LANES), F32)],
        compiler_params=_params("parallel", "arbitrary"),
        name="delta_local",
    )(qkv, qkv, qkv, p_small, alog_row, dtb_row)


def _delta_scan_body(uf_ref, wf_ref, qf_ref, af_ref, ktf_ref, ef_ref,
                     ub_ref, wb_ref, qb_ref, ab_ref, ktb_ref, eb_ref,
                     of_ref, ob_ref, s_scr):
    @pl.when(pl.program_id(1) == 0)
    def _():
        s_scr[...] = jnp.zeros_like(s_scr)

    chains = ((uf_ref, wf_ref, qf_ref, af_ref, ktf_ref, ef_ref, of_ref),
              (ub_ref, wb_ref, qb_ref, ab_ref, ktb_ref, eb_ref, ob_ref))
    for d, (u_ref, w_ref, q_ref, a_ref, kt_ref, e_ref, o_ref) in enumerate(chains):
        for h in range(N_HEADS):
            hs = slice(h * HEAD_DIM, (h + 1) * HEAD_DIM)
            ds_ = slice(d * CHUNK, (d + 1) * CHUNK)
            state = s_scr[d * N_HEADS + h]
            wq = jnp.concatenate([w_ref[:, hs], q_ref[:, hs]], axis=0)
            x = _dot(wq, state.astype(BF16))
            v_new = (u_ref[:, hs] - x[:CHUNK]).astype(BF16)
            y = _dot(jnp.concatenate([a_ref[:, hs][:, ds_], kt_ref[:, hs][:, ds_]], axis=0), v_new)
            o_ref[:, hs] = x[CHUNK:] + y[:CHUNK]
            s_scr[d * N_HEADS + h] = state * e_ref[0:1, hs] + y[CHUNK:]


def _delta_scan(local_out, bsz, seq):
    uf, ub, wf, wb, qf, qb, a, kt, ef, eb = local_out
    n, cols = uf.shape
    n_chunks = seq // CHUNK
    fwd = lambda b, c: (b * n_chunks + c, 0)
    bwd = lambda b, c: (b * n_chunks + (n_chunks - 1 - c), 0)
    specs = lambda idx: [pl.BlockSpec((CHUNK, cols), idx), pl.BlockSpec((CHUNK, cols), idx),
                         pl.BlockSpec((CHUNK, cols), idx), pl.BlockSpec((CHUNK, cols), idx),
                         pl.BlockSpec((2 * CHUNK, cols), idx), pl.BlockSpec((SUBLANES, cols), idx)]
    return pl.pallas_call(
        _delta_scan_body,
        grid=(bsz, n_chunks),
        in_specs=specs(fwd) + specs(bwd),
        out_specs=[pl.BlockSpec((CHUNK, cols), fwd), pl.BlockSpec((CHUNK, cols), bwd)],
        out_shape=[jax.ShapeDtypeStruct((n, cols), F32), jax.ShapeDtypeStruct((n, cols), F32)],
        scratch_shapes=[pltpu.VMEM((2 * N_HEADS, HEAD_DIM, HEAD_DIM), F32)],
        compiler_params=_params("parallel", "arbitrary"),
        name="delta_scan",
    )(uf, wf, qf, a, kt, ef, ub, wb, qb, a, kt, eb)


def _mixer_out_body(of_ref, ob_ref, z_ref, yb_ref, ga_ref, gb_ref, x_ref,
                    onw_ref, wa_ref, wb_ref, wo_ref, gtm_ref, n2w_ref, scf_ref, shf_ref,
                    rw_ref, rb_ref, x1_ref, h2_ref, lg_ref, oz_scr):
    o = of_ref[...] + ob_ref[...]
    z = z_ref[...]
    onw = onw_ref[...]
    for h in range(N_HEADS):
        hs = slice(h * HEAD_DIM, (h + 1) * HEAD_DIM)
        oh = o[:, hs]
        ms = jnp.mean(oh * oh, axis=-1, keepdims=True)
        zh = z[:, hs]
        oz_scr[:, hs] = (oh * lax.rsqrt(ms + EPS) * onw * (zh * _sigmoid(zh))).astype(BF16)
    y_a = _dot(oz_scr[...], wa_ref[...])
    y_b = _dot(yb_ref[...].astype(BF16), wb_ref[...])
    merged = _sigmoid(ga_ref[...]) * y_a + _sigmoid(gb_ref[...]) * y_b
    x1 = x_ref[...] + gtm_ref[0] * _dot(merged.astype(BF16), wo_ref[...])
    x1_ref[...] = x1
    ms = jnp.mean(x1 * x1, axis=-1, keepdims=True)
    h2 = x1 * lax.rsqrt(ms + EPS) * n2w_ref[...] * (1.0 + scf_ref[0]) + shf_ref[0]
    h2_ref[...] = h2
    lg_ref[...] = jnp.dot(h2, rw_ref[...], preferred_element_type=F32,
                          precision=lax.Precision.HIGHEST) + rb_ref[...]


def _mixer_out(o_f, o_b, p_main, y_sc, x2d, onorm_w, w_up_a, w_out_sc, w_o, gt_m, norm2_w,
               sc_f, sh_f, router_w, router_b, seq, z_blk, ga_blk):
    n, d = x2d.shape
    tm = min(256, seq)
    tiles_per_batch = seq // tm
    row = lambda i: (i, 0)
    const = lambda i: (0, 0)
    per_b = lambda i: (i // tiles_per_batch, 0, 0)
    rw = jnp.pad(router_w, ((0, 0), (0, LANES - N_EXPERTS)))
    rb = jnp.pad(router_b.reshape(1, -1), ((0, 0), (0, LANES - N_EXPERTS)))
    return pl.pallas_call(
        _mixer_out_body,
        grid=(n // tm,),
        in_specs=[pl.BlockSpec((tm, d), row), pl.BlockSpec((tm, d), row),
                  pl.BlockSpec((tm, d), lambda i: (i, z_blk)), pl.BlockSpec((tm, d), row),
                  pl.BlockSpec((tm, d), lambda i: (i, ga_blk)),
                  pl.BlockSpec((tm, d), lambda i: (i, ga_blk + 1)), pl.BlockSpec((tm, d), row),
                  pl.BlockSpec((1, HEAD_DIM), const), pl.BlockSpec((d, d), const),
                  pl.BlockSpec((d, d), const), pl.BlockSpec((d, d), const),
                  pl.BlockSpec((1, 1, d), per_b), pl.BlockSpec((1, d), const),
                  pl.BlockSpec((1, 1, d), per_b), pl.BlockSpec((1, 1, d), per_b),
                  pl.BlockSpec((d, LANES), const), pl.BlockSpec((1, LANES), const)],
        out_specs=[pl.BlockSpec((tm, d), row), pl.BlockSpec((tm, d), row),
                   pl.BlockSpec((tm, LANES), row)],
        out_shape=[jax.ShapeDtypeStruct((n, d), F32), jax.ShapeDtypeStruct((n, d), F32),
                   jax.ShapeDtypeStruct((n, LANES), F32)],
        scratch_shapes=[pltpu.VMEM((tm, d), BF16)],
        compiler_params=_params("parallel"),
        name="mixer_out",
    )(o_f, o_b, p_main, y_sc, p_main, p_main, x2d, onorm_w.reshape(1, HEAD_DIM),
      w_up_a.astype(BF16), w_out_sc.astype(BF16), w_o.astype(BF16), gt_m,
      norm2_w.reshape(1, d), sc_f, sh_f, rw, rb)


def _route_body(lg_ref, er_ref, gate_ref, cnt_ref, carry_scr):
    @pl.when(pl.program_id(0) == 0)
    def _():
        carry_scr[...] = jnp.zeros_like(carry_scr)

    rows = lg_ref.shape[0]
    lane = lax.broadcasted_iota(jnp.int32, (rows, LANES), 1)
    lanef = lane.astype(F32)
    logits = jnp.where(lane < N_EXPERTS, lg_ref[...], -jnp.inf)
    vals, idxs = [], []
    sel = jnp.zeros((rows, LANES), F32)
    for _ in range(TOP_K):
        m = jnp.max(logits, axis=-1, keepdims=True)
        idx = jnp.min(jnp.where(logits == m, lanef, float(LANES)), axis=-1, keepdims=True)
        hit = lanef == idx
        vals.append(m)
        idxs.append(idx)
        sel = sel + hit.astype(F32)
        logits = jnp.where(hit, -jnp.inf, logits)
    exps = [jnp.exp(v - vals[0]) for v in vals]
    total = exps[0]
    for e in exps[1:]:
        total = total + e
    r = lax.broadcasted_iota(jnp.int32, (rows, rows), 0)
    c = lax.broadcasted_iota(jnp.int32, (rows, rows), 1)
    earlier = (r > c).astype(BF16)
    carry = carry_scr[0:1, :]
    before = _dot(earlier, sel.astype(BF16)) + carry
    er = jnp.zeros((rows, LANES), F32)
    gates = jnp.zeros((rows, LANES), F32)
    for k in range(TOP_K):
        rank = jnp.sum(jnp.where(lanef == idxs[k], before, 0.0), axis=-1, keepdims=True)
        er = jnp.where(lane == k, idxs[k], er)
        er = jnp.where(lane == TOP_K + k, rank, er)
        gates = jnp.where(lane == k, exps[k] / total, gates)
    er_ref[...] = er
    gate_ref[...] = gates
    new_carry = carry + jnp.sum(sel, axis=0, keepdims=True)
    carry_scr[...] = jnp.broadcast_to(new_carry, carry_scr.shape)
    cnt_ref[...] = jnp.broadcast_to(new_carry, cnt_ref.shape)


def _route(logits):
    n = logits.shape[0]
    rows = min(256, n)
    tile = pl.BlockSpec((rows, LANES), lambda i: (i, 0))
    return pl.pallas_call(
        _route_body,
        grid=(n // rows,),
        in_specs=[tile],
        out_specs=[tile, tile, pl.BlockSpec((SUBLANES, LANES), lambda i: (0, 0))],
        out_shape=[jax.ShapeDtypeStruct((n, LANES), F32), jax.ShapeDtypeStruct((n, LANES), F32),
                   jax.ShapeDtypeStruct((SUBLANES, LANES), F32)],
        scratch_shapes=[pltpu.VMEM((SUBLANES, LANES), F32)],
        compiler_params=_params("arbitrary"),
        name="route",
    )(logits)


def _dest_body(er_ref, ps_ref, o_ref):
    rows = er_ref.shape[0]
    lane = lax.broadcasted_iota(jnp.int32, (rows, LANES), 1)
    lanef = lane.astype(F32)
    er = er_ref[...]
    starts = ps_ref[0:1, :]
    out = jnp.zeros((rows, LANES), F32)
    for k in range(TOP_K):
        e_k = jnp.sum(jnp.where(lane == k, er, 0.0), axis=-1, keepdims=True)
        r_k = jnp.sum(jnp.where(lane == TOP_K + k, er, 0.0), axis=-1, keepdims=True)
        base = jnp.sum(jnp.where(lanef == e_k, starts, 0.0), axis=-1, keepdims=True)
        out = jnp.where(lane == k, base + r_k, out)
    o_ref[...] = out.astype(jnp.int32)


def _dest_rows(er, pad_start_row):
    n = er.shape[0]
    rows = min(1024, n)
    tile = pl.BlockSpec((rows, LANES), lambda i: (i, 0))
    return pl.pallas_call(
        _dest_body,
        grid=(n // rows,),
        in_specs=[tile, pl.BlockSpec((SUBLANES, LANES), lambda i: (0, 0))],
        out_specs=tile,
        out_shape=jax.ShapeDtypeStruct((n, LANES), jnp.int32),
        compiler_params=_params("parallel"),
        name="dest_rows",
    )(er, pad_start_row)


def _row_copy(src, src_row, dst, dst_row, sem):
    return pltpu.make_async_copy(src.at[pl.ds(src_row, 1)], dst.at[pl.ds(dst_row, 1)], sem)


def _dispatch_body(dest_ref, h_hbm, xs_init_hbm, xs_hbm, sem):
    del xs_init_hbm
    n_pairs = dest_ref.shape[-1]
    base = pl.program_id(0) * (n_pairs // TOP_K)

    def issue(p, carry):
        _row_copy(h_hbm, base + p // TOP_K, xs_hbm, dest_ref[0, 0, p], sem).start()
        return carry

    def drain(p, carry):
        _row_copy(h_hbm, 0, xs_hbm, 0, sem).wait()
        return carry

    lax.fori_loop(0, n_pairs, issue, 0)
    lax.fori_loop(0, n_pairs, drain, 0)


def _dispatch(dest_flat, h2, m_pad):
    n, d = h2.shape
    rows = min(256, n)
    dest3 = dest_flat.reshape(n // rows, 1, rows * TOP_K)
    xs_init = jnp.zeros((m_pad, d), h2.dtype)
    return pl.pallas_call(
        _dispatch_body,
        grid=(n // rows,),
        in_specs=[pl.BlockSpec((1, 1, rows * TOP_K), lambda i: (i, 0, 0),
                               memory_space=pltpu.SMEM),
                  pl.BlockSpec(memory_space=pl.ANY), pl.BlockSpec(memory_space=pl.ANY)],
        out_specs=pl.BlockSpec(memory_space=pl.ANY),
        out_shape=jax.ShapeDtypeStruct((m_pad, d), h2.dtype),
        scratch_shapes=[pltpu.SemaphoreType.DMA],
        input_output_aliases={2: 0},
        compiler_params=_params("arbitrary"),
        name="dispatch",
    )(dest3, h2, xs_init)


def _expert_body(be_ref, nu_ref, x_ref, w1_ref, b1_ref, w2_ref, b2_ref, o_ref):
    del be_ref
    i = pl.program_id(0)

    @pl.when(i < nu_ref[0])
    def _():
        d_ff = w2_ref.shape[1]
        gu = _dot(x_ref[...].astype(BF16), w1_ref[0]) + b1_ref[0]
        gate = jnp.minimum(gu[:, :d_ff], SWIGLU_LIMIT)
        up = jnp.clip(gu[:, d_ff:], -SWIGLU_LIMIT, SWIGLU_LIMIT)
        act = gate * _sigmoid(SWIGLU_ALPHA * gate) * (up + 1.0)
        o_ref[...] = _dot(act.astype(BF16), w2_ref[0]) + b2_ref[0]

    @pl.when(i >= nu_ref[0])
    def _():
        o_ref[...] = jnp.zeros_like(o_ref)


def _experts(xs, blk_exp, n_used, w1, b1, w2, b2):
    m_pad, d = xs.shape
    n_exp, _, two_ff = w1.shape
    d_ff = two_ff // 2
    grid_spec = pltpu.PrefetchScalarGridSpec(
        num_scalar_prefetch=2,
        grid=(m_pad // MOE_ROWS,),
        in_specs=[pl.BlockSpec((MOE_ROWS, d), lambda i, be, nu: (i, 0)),
                  pl.BlockSpec((1, d, two_ff), lambda i, be, nu: (be[i], 0, 0)),
                  pl.BlockSpec((1, 1, two_ff), lambda i, be, nu: (be[i], 0, 0)),
                  pl.BlockSpec((1, d_ff, d), lambda i, be, nu: (be[i], 0, 0)),
                  pl.BlockSpec((1, 1, d), lambda i, be, nu: (be[i], 0, 0))],
        out_specs=pl.BlockSpec((MOE_ROWS, d), lambda i, be, nu: (i, 0)),
    )
    return pl.pallas_call(
        _expert_body,
        grid_spec=grid_spec,
        out_shape=jax.ShapeDtypeStruct((m_pad, d), F32),
        compiler_params=_params("arbitrary"),
        name="experts",
    )(blk_exp, n_used, xs, w1.astype(BF16), b1.reshape(n_exp, 1, two_ff), w2.astype(BF16),
      b2.reshape(n_exp, 1, d))


def _combine_body(dest_ref, y_hbm, x1_ref, gate_ref, gtf_ref, fw_ref, o_ref, buf, sem):
    rows = x1_ref.shape[0]

    def issue(t, carry):
        for k in range(TOP_K):
            pltpu.make_async_copy(y_hbm.at[pl.ds(dest_ref[0, 0, t * TOP_K + k], 1)],
                                  buf.at[k, pl.ds(t, 1)], sem).start()
        return carry

    def drain(t, carry):
        for k in range(TOP_K):
            pltpu.make_async_copy(y_hbm.at[pl.ds(0, 1)], buf.at[k, pl.ds(0, 1)], sem).wait()
        return carry

    lax.fori_loop(0, rows, issue, 0)
    lax.fori_loop(0, rows, drain, 0)
    gates = gate_ref[...]
    moe = gates[:, 0:1] * buf[0]
    for k in range(1, TOP_K):
        moe = moe + gates[:, k:k + 1] * buf[k]
    x2 = x1_ref[...] + gtf_ref[0] * moe
    ms = jnp.mean(x2 * x2, axis=-1, keepdims=True)
    o_ref[...] = x2 * lax.rsqrt(ms + EPS) * fw_ref[...]


def _combine(dest_flat, yb, x1, gates, gt_f, final_w, seq):
    n, d = x1.shape
    rows = min(128, seq)
    tiles_per_batch = seq // rows
    dest3 = dest_flat.reshape(n // rows, 1, rows * TOP_K)
    return pl.pallas_call(
        _combine_body,
        grid=(n // rows,),
        in_specs=[pl.BlockSpec((1, 1, rows * TOP_K), lambda i: (i, 0, 0),
                               memory_space=pltpu.SMEM),
                  pl.BlockSpec(memory_space=pl.ANY),
                  pl.BlockSpec((rows, d), lambda i: (i, 0)),
                  pl.BlockSpec((rows, LANES), lambda i: (i, 0)),
                  pl.BlockSpec((1, 1, d), lambda i: (i // tiles_per_batch, 0, 0)),
                  pl.BlockSpec((1, d), lambda i: (0, 0))],
        out_specs=pl.BlockSpec((rows, d), lambda i: (i, 0)),
        out_shape=jax.ShapeDtypeStruct((n, d), F32),
        scratch_shapes=[pltpu.VMEM((TOP_K, rows, d), F32), pltpu.SemaphoreType.DMA],
        compiler_params=_params("arbitrary"),
        name="combine",
    )(dest3, yb, x1, gates, gt_f, final_w.reshape(1, d))


def _layer(x2d, c_mod, bsz, seq, norm1_w, w_in, conv_qkv_w, a_log, dt_bias, onorm_w, w_up_a,
           conv_sc_w, w_out_sc, w_o, norm2_w, router_w, router_b, moe_w1, moe_b1, moe_w2, moe_b2):
    n, d = x2d.shape
    dn = N_HEADS * HEAD_DIM
    sh_m, sc_m, gt_m, sh_f, sc_f, gt_f = (m.reshape(bsz, 1, d) for m in jnp.split(c_mod, 6, axis=-1))

    o_qkv, o_z, o_small, o_sc, o_gates = 0, 3 * dn, 4 * dn, 4 * dn + 4 * N_HEADS, 4 * dn + 4 * N_HEADS + 3 * d
    w_main = jnp.concatenate([w_in[:, o_qkv:o_small], w_in[:, o_sc:]], axis=1).astype(BF16)
    w_small = jnp.pad(w_in[:, o_small:o_sc], ((0, 0), (0, LANES - 4 * N_HEADS))).astype(BF16)
    del o_z, o_gates
    p_main, p_small = _in_projection(x2d, norm1_w, sc_m, sh_m, w_main, w_small, seq)

    qkv = _qkv_conv(p_main, conv_qkv_w, seq)
    y_sc = _sc_conv(p_main, conv_sc_w, seq, first_col_block=4 * dn // d)
    local = _delta_local(qkv, p_small, a_log, dt_bias, seq)
    o_f, o_b = _delta_scan(local, bsz, seq)
    x1, h2, logits = _mixer_out(o_f, o_b, p_main, y_sc, x2d, onorm_w, w_up_a, w_out_sc, w_o, gt_m,
                                norm2_w, sc_f, sh_f, router_w, router_b, seq,
                                z_blk=3 * dn // d, ga_blk=(4 * dn + 3 * d) // d)

    er, gates, counts = _route(logits)
    m_rows = n * TOP_K
    n_blocks = m_rows // MOE_ROWS + N_EXPERTS
    cnt = counts[0, :N_EXPERTS].astype(jnp.int32)
    padded = (cnt + MOE_ROWS - 1) // MOE_ROWS * MOE_ROWS
    pad_end = jnp.cumsum(padded)
    pad_start = pad_end - padded
    blk_exp = jnp.minimum(
        jnp.searchsorted(pad_end, jnp.arange(n_blocks, dtype=jnp.int32) * MOE_ROWS, side="right"),
        N_EXPERTS - 1).astype(jnp.int32)
    n_used = (pad_end[-1:] // MOE_ROWS).astype(jnp.int32)
    ps_row = jnp.broadcast_to(
        jnp.pad(pad_start.astype(F32), (0, LANES - N_EXPERTS)).reshape(1, LANES), (SUBLANES, LANES))
    dest = _dest_rows(er, ps_row)[:, :TOP_K].reshape(m_rows)

    xs = _dispatch(dest, h2, n_blocks * MOE_ROWS)
    yb = _experts(xs, blk_exp, n_used, moe_w1, moe_b1, moe_w2, moe_b2)
    return dest, yb, x1, gates, gt_f


def kernel(x, c, ada_w, ada_b, norm1_w, w_in, conv_qkv_w, a_log, dt_bias, onorm_w, w_up_a, conv_sc_w, w_out_sc, w_o, norm2_w, router_w, router_b, moe_w1, moe_b1, moe_w2, moe_b2, final_norm_w):
    bsz, seq, d = x.shape
    depth = ada_w.shape[0]
    assert depth == 1, "the combine stage fuses the final norm, which needs a single layer"
    x2d = x.reshape(bsz * seq, d)
    c_mod = _ada_mod(c, ada_w[0], ada_b[0])
    dest, yb, x1, gates, gt_f = _layer(
        x2d, c_mod, bsz, seq, norm1_w[0], w_in[0], conv_qkv_w[0], a_log[0], dt_bias[0], onorm_w[0],
        w_up_a[0], conv_sc_w[0], w_out_sc[0], w_o[0], norm2_w[0], router_w[0], router_b[0],
        moe_w1[0], moe_b1[0], moe_w2[0], moe_b2[0])
    out = _combine(dest, yb, x1, gates, gt_f, final_norm_w, seq)
    return out.reshape(bsz, seq, d)
```

```python
import functools

import jax
import jax.numpy as jnp
from jax import lax
from jax.experimental import pallas as pl
from jax.experimental.pallas import tpu as pltpu

F32 = jnp.float32
BF16 = jnp.bfloat16

EPS = 1e-6
N_HEADS = 8
HEAD_DIM = 128
CHUNK = 64
N_EXPERTS = 32
TOP_K = 4
SWIGLU_LIMIT = 7.0
SWIGLU_ALPHA = 1.702
LANES = 128
SUBLANES = 8
MOE_ROWS = 256
VMEM_LIMIT = 56 * 1024 * 1024


def _sigmoid(x):
    return 1.0 / (1.0 + jnp.exp(-x))


def _softplus(x):
    return jnp.maximum(x, 0.0) + jnp.log(1.0 + jnp.exp(-jnp.abs(x)))


def _dot(a, b):
    return jnp.dot(a, b, preferred_element_type=F32)


def _bmm(a, b):
    return lax.dot_general(a, b, (((2,), (1,)), ((0,), (0,))), preferred_element_type=F32)


def _bmm_nt(a, b):
    return lax.dot_general(a, b, (((2,), (2,)), ((0,), (0,))), preferred_element_type=F32)


def _params(*sem):
    return pltpu.CompilerParams(dimension_semantics=sem, vmem_limit_bytes=VMEM_LIMIT)


def _rows_to_tiles(ref, value):
    n_rows = value.shape[0]
    for s in range(SUBLANES):
        ref[pl.ds(s, n_rows, stride=SUBLANES), :] = value[:, s * LANES:(s + 1) * LANES]


def _tiles_to_rows(ref, n_rows):
    return jnp.concatenate(
        [ref[pl.ds(s, n_rows, stride=SUBLANES), :] for s in range(SUBLANES)], axis=1)


def _ada_body(c_ref, w_ref, b_ref, o_ref):
    c = c_ref[...]
    ca = c * _sigmoid(c)
    o_ref[...] = _dot(ca.astype(BF16), w_ref[...].astype(BF16)) + b_ref[...]


def _ada_mod(c, ada_w, ada_b):
    bsz, d = c.shape
    n_out = ada_w.shape[1]
    cp = jnp.pad(c, ((0, SUBLANES - bsz), (0, 0)))
    out = pl.pallas_call(
        _ada_body,
        grid=(n_out // d,),
        in_specs=[pl.BlockSpec((SUBLANES, d), lambda j: (0, 0)),
                  pl.BlockSpec((d, d), lambda j: (0, j)),
                  pl.BlockSpec((1, d), lambda j: (0, j))],
        out_specs=pl.BlockSpec((SUBLANES, d), lambda j: (0, j)),
        out_shape=jax.ShapeDtypeStruct((SUBLANES, n_out), F32),
        compiler_params=_params("parallel"),
        name="ada_mod",
    )(cp, ada_w, ada_b.reshape(1, n_out))
    return out[:bsz]


def _inproj_body(x_ref, nw_ref, sc_ref, sh_ref, w_ref, ws_ref, o_ref, os_ref, h_scr):
    @pl.when(pl.program_id(1) == 0)
    def _():
        x = x_ref[...]
        ms = jnp.mean(x * x, axis=-1, keepdims=True)
        xn = x * lax.rsqrt(ms + EPS) * nw_ref[...]
        h = (xn * (1.0 + sc_ref[0]) + sh_ref[0]).astype(BF16)
        h_scr[...] = h
        os_ref[...] = _dot(h, ws_ref[...])

    o_ref[...] = _dot(h_scr[...], w_ref[...])


def _in_projection(x2d, norm_w, sc_m, sh_m, w_main, w_small, seq):
    n, d = x2d.shape
    width = w_main.shape[1]
    tm = min(1024, seq)
    tn = 1024
    tiles_per_batch = seq // tm
    return pl.pallas_call(
        _inproj_body,
        grid=(n // tm, width // tn),
        in_specs=[pl.BlockSpec((tm, d), lambda i, j: (i, 0)),
                  pl.BlockSpec((1, d), lambda i, j: (0, 0)),
                  pl.BlockSpec((1, 1, d), lambda i, j: (i // tiles_per_batch, 0, 0)),
                  pl.BlockSpec((1, 1, d), lambda i, j: (i // tiles_per_batch, 0, 0)),
                  pl.BlockSpec((d, tn), lambda i, j: (0, j)),
                  pl.BlockSpec((d, LANES), lambda i, j: (0, 0))],
        out_specs=[pl.BlockSpec((tm, tn), lambda i, j: (i, j)),
                   pl.BlockSpec((tm, LANES), lambda i, j: (i, 0))],
        out_shape=[jax.ShapeDtypeStruct((n, width), F32),
                   jax.ShapeDtypeStruct((n, LANES), F32)],
        scratch_shapes=[pltpu.VMEM((tm, d), BF16)],
        compiler_params=_params("parallel", "arbitrary"),
        name="in_proj",
    )(x2d, norm_w.reshape(1, d), sc_m, sh_m, w_main, w_small)


def _conv_rows(ext, cw, width, rows):
    length = rows + 2 * SUBLANES
    pad = width // 2
    acc = ext * cw[pad:pad + 1]
    for w in range(width):
        if w != pad:
            acc = acc + pltpu.roll(ext, (pad - w) % length, axis=0) * cw[w:w + 1]
    return acc[SUBLANES:SUBLANES + rows]


def _qkv_conv_body(cur_ref, prev_ref, next_ref, cw_ref, o_ref, *, tiles_per_batch, width,
                   n_norm_blocks):
    t_in_b = pl.program_id(0) % tiles_per_batch
    rows = cur_ref.shape[0]
    prev = jnp.where(t_in_b == 0, 0.0, prev_ref[...])
    nxt = jnp.where(t_in_b == tiles_per_batch - 1, 0.0, next_ref[...])
    ext = jnp.concatenate([prev, cur_ref[...], nxt], axis=0)
    y = _conv_rows(ext, cw_ref[...], width, rows)
    y = y * _sigmoid(y)
    normed = pl.program_id(1) < n_norm_blocks
    for h in range(y.shape[1] // HEAD_DIM):
        blk = y[:, h * HEAD_DIM:(h + 1) * HEAD_DIM]
        ss = jnp.sum(blk * blk, axis=-1, keepdims=True)
        scale = jnp.where(normed, lax.rsqrt(ss + EPS), 1.0)
        o_ref[:, h * HEAD_DIM:(h + 1) * HEAD_DIM] = blk * scale


def _halo_specs(rows, cols, col_block, n_rows):
    per = rows // SUBLANES
    last = n_rows // SUBLANES - 1
    cur = pl.BlockSpec((rows, cols), lambda i, j: (i, col_block(j)))
    prev = pl.BlockSpec((SUBLANES, cols), lambda i, j: (jnp.maximum(i * per - 1, 0), col_block(j)))
    nxt = pl.BlockSpec((SUBLANES, cols), lambda i, j: (jnp.minimum((i + 1) * per, last), col_block(j)))
    return cur, prev, nxt


def _qkv_conv(p_main, conv_w, seq):
    n = p_main.shape[0]
    width, ch = conv_w.shape
    cols = N_HEADS * HEAD_DIM
    rows = min(512, seq)
    cw = jnp.pad(conv_w, ((0, SUBLANES - width), (0, 0)))
    cur, prev, nxt = _halo_specs(rows, cols, lambda j: j, n)
    body = functools.partial(_qkv_conv_body, tiles_per_batch=seq // rows, width=width,
                             n_norm_blocks=2)
    return pl.pallas_call(
        body,
        grid=(n // rows, ch // cols),
        in_specs=[cur, prev, nxt, pl.BlockSpec((SUBLANES, cols), lambda i, j: (0, j))],
        out_specs=pl.BlockSpec((rows, cols), lambda i, j: (i, j)),
        out_shape=jax.ShapeDtypeStruct((n, ch), F32),
        compiler_params=_params("parallel", "parallel"),
        name="qkv_conv",
    )(p_main, p_main, p_main, cw)


def _sc_conv_body(b_ref, c_ref, cp_ref, cn_ref, u_ref, up_ref, un_ref, cw_ref, o_ref, *,
                  tiles_per_batch, width):
    t_in_b = pl.program_id(0) % tiles_per_batch
    rows = c_ref.shape[0]
    prev = jnp.where(t_in_b == 0, 0.0, cp_ref[...] * up_ref[...])
    nxt = jnp.where(t_in_b == tiles_per_batch - 1, 0.0, cn_ref[...] * un_ref[...])
    ext = jnp.concatenate([prev, c_ref[...] * u_ref[...], nxt], axis=0)
    o_ref[...] = b_ref[...] * _conv_rows(ext, cw_ref[...], width, rows)


def _sc_conv(p_main, conv_w, seq, first_col_block):
    n = p_main.shape[0]
    width, ch = conv_w.shape
    rows = min(512, seq)
    cw = jnp.pad(conv_w, ((0, SUBLANES - width), (0, 0)))
    b_spec = pl.BlockSpec((rows, ch), lambda i, j: (i, first_col_block))
    c_specs = _halo_specs(rows, ch, lambda j: first_col_block + 1, n)
    u_specs = _halo_specs(rows, ch, lambda j: first_col_block + 2, n)
    body = functools.partial(_sc_conv_body, tiles_per_batch=seq // rows, width=width)
    return pl.pallas_call(
        body,
        grid=(n // rows, 1),
        in_specs=[b_spec, *c_specs, *u_specs, pl.BlockSpec((SUBLANES, ch), lambda i, j: (0, 0))],
        out_specs=pl.BlockSpec((rows, ch), lambda i, j: (i, 0)),
        out_shape=jax.ShapeDtypeStruct((n, ch), F32),
        compiler_params=_params("parallel", "arbitrary"),
        name="sc_conv",
    )(p_main, p_main, p_main, p_main, p_main, p_main, p_main, cw)


def _block_diag(x, lo_half):
    return jnp.concatenate([jnp.where(lo_half, x, 0.0), jnp.where(lo_half, 0.0, x)], axis=1)


def _delta_local_body(q_ref, k_ref, v_ref, s_ref, alog_ref, dtb_ref,
                      uf_ref, ub_ref, wf_ref, wb_ref, qf_ref, qb_ref, a_ref, kt_ref, ef_ref, eb_ref,
                      beta_scr, g_scr, gl_scr, gt_scr, gtr_scr):
    h = pl.program_id(1)
    rows = q_ref.shape[0]
    n_chunks = rows // CHUNK
    lane = lax.broadcasted_iota(jnp.int32, (rows, LANES), 1)

    @pl.when(h == 0)
    def _():
        s = s_ref[...]
        beta_scr[...] = _sigmoid(s)
        g = -jnp.exp(alog_ref[...]) * _softplus(s + dtb_ref[...])
        pos = lax.broadcasted_iota(jnp.int32, (rows, LANES), 0) & (CHUNK - 1)
        gp = g
        gs = g
        step = 1
        while step < CHUNK:
            gp = gp + jnp.where(pos >= step, pltpu.roll(gp, step, axis=0), 0.0)
            gs = gs + jnp.where(pos < CHUNK - step, pltpu.roll(gs, rows - step, axis=0), 0.0)
            step *= 2
        gc = jnp.where(lane >= 3 * N_HEADS, gs, gp)
        g_scr[...] = gc
        gl_scr[...] = gp + gs - g
        for c2 in range(rows // LANES):
            gt = gc[c2 * LANES:(c2 + 1) * LANES, :].T
            gt_scr[c2] = gt
            gtr_scr[c2] = pltpu.roll(gt, CHUNK, axis=1)

    def pick(ref, idx):
        return jnp.sum(jnp.where(lane == idx, ref[...], 0.0), axis=-1, keepdims=True)

    i_f, i_b = 2 * N_HEADS + h, 3 * N_HEADS + h
    beta_f, beta_b = pick(beta_scr, h), pick(beta_scr, N_HEADS + h)
    g_f, g_b = pick(g_scr, i_f), pick(g_scr, i_b)
    gl_f, gl_b = pick(gl_scr, i_f), pick(gl_scr, i_b)

    k2 = k_ref[...]
    q2 = q_ref[...] * (HEAD_DIM ** -0.5)
    v2 = v_ref[...]
    kb_f, kb_b = k2 * beta_f, k2 * beta_b
    eg_f, eg_b = jnp.exp(g_f), jnp.exp(g_b)
    to3 = lambda a: a.reshape(n_chunks, CHUNK, a.shape[-1])
    bf3 = lambda a: to3(a.astype(BF16))

    k3 = bf3(k2)
    zero3 = jnp.zeros_like(k3)
    lhs = jnp.concatenate([jnp.concatenate([bf3(kb_f), bf3(kb_b)], axis=2),
                           jnp.concatenate([bf3(q2), bf3(q2)], axis=2)], axis=1)
    rhs_t = jnp.concatenate([jnp.concatenate([k3, zero3], axis=2),
                             jnp.concatenate([zero3, k3], axis=2)], axis=1)
    m1 = _bmm_nt(lhs, rhs_t)

    ii = lax.broadcasted_iota(jnp.int32, (CHUNK, LANES), 0)
    l2 = lax.broadcasted_iota(jnp.int32, (CHUNK, LANES), 1)
    jj = l2 & (CHUNK - 1)
    lo_half = l2 < CHUNK
    ahead = jnp.where(lo_half, ii - jj, jj - ii)
    incl = ahead >= 0
    strict = ahead > 0
    eye = (ii == jj).astype(F32)

    g_cols = jnp.where(lane < CHUNK, g_f, g_b)
    lrow = lax.broadcasted_iota(jnp.int32, (1, LANES), 1) < CHUNK
    diffs = []
    for c in range(n_chunks):
        rf = gt_scr[c // 2, pl.ds(i_f, 1), :]
        rb = gt_scr[c // 2, pl.ds(i_b, 1), :]
        if c % 2 == 0:
            g_row = jnp.where(lrow, rf, gtr_scr[c // 2, pl.ds(i_b, 1), :])
        else:
            g_row = jnp.where(lrow, gtr_scr[c // 2, pl.ds(i_f, 1), :], rb)
        diffs.append(g_cols[c * CHUNK:(c + 1) * CHUNK] - g_row)
    decay = jnp.exp(jnp.where(incl, jnp.stack(diffs), -jnp.inf))

    nm = -jnp.where(strict, m1[:, :CHUNK] * decay, 0.0)
    a_ref[...] = (m1[:, CHUNK:] * decay).astype(BF16).reshape(rows, LANES)

    p = eye + nm
    npow = _bmm(nm.astype(BF16), _block_diag(nm, lo_half).astype(BF16))
    n_iter = CHUNK.bit_length() - 2
    for it in range(n_iter):
        bd = _block_diag(npow, lo_half).astype(BF16)
        if it + 1 < n_iter:
            x = _bmm(jnp.concatenate([p, npow], axis=1).astype(BF16), bd)
            p = p + x[:, :CHUNK]
            npow = x[:, CHUNK:]
        else:
            p = p + _bmm(p.astype(BF16), bd)

    rhs = jnp.concatenate([jnp.concatenate([bf3(v2 * beta_f), bf3(kb_f * eg_f)], axis=2),
                           jnp.concatenate([bf3(v2 * beta_b), bf3(kb_b * eg_b)], axis=2)], axis=1)
    uw = _bmm(_block_diag(p, lo_half).astype(BF16), rhs)
    uf_ref[...] = uw[:, :CHUNK, :HEAD_DIM].reshape(rows, HEAD_DIM)
    ub_ref[...] = uw[:, CHUNK:, :HEAD_DIM].reshape(rows, HEAD_DIM)
    wf_ref[...] = uw[:, :CHUNK, HEAD_DIM:].astype(BF16).reshape(rows, HEAD_DIM)
    wb_ref[...] = uw[:, CHUNK:, HEAD_DIM:].astype(BF16).reshape(rows, HEAD_DIM)
    qf_ref[...] = (q2 * eg_f).astype(BF16)
    qb_ref[...] = (q2 * eg_b).astype(BF16)

    kd_f = to3(k2 * jnp.exp(gl_f - g_f))
    kd_b = to3(k2 * jnp.exp(gl_b - g_b))
    e_f = to3(jnp.broadcast_to(jnp.exp(gl_f), (rows, LANES)))
    e_b = to3(jnp.broadcast_to(jnp.exp(gl_b), (rows, LANES)))
    ef_ref[...] = e_f[:, :SUBLANES, :].reshape(n_chunks * SUBLANES, LANES)
    eb_ref[...] = e_b[:, :SUBLANES, :].reshape(n_chunks * SUBLANES, LANES)
    for c in range(n_chunks):
        kt_ref[c * 2 * CHUNK:(c + 1) * 2 * CHUNK, :] = (
            jnp.concatenate([kd_f[c], kd_b[c]], axis=0).T.astype(BF16))


def _delta_local(qkv, p_small, a_log, dt_bias, seq):
    n = qkv.shape[0]
    cols = N_HEADS * HEAD_DIM
    rows = min(512, seq)
    alog_row = jnp.pad(a_log.reshape(1, -1), ((0, 0), (2 * N_HEADS, LANES - 4 * N_HEADS)))
    dtb_row = jnp.pad(dt_bias.reshape(1, -1), ((0, 0), (2 * N_HEADS, LANES - 4 * N_HEADS)))
    blk = lambda off: pl.BlockSpec((rows, HEAD_DIM), lambda i, h: (i, off + h))
    row_spec = pl.BlockSpec((1, LANES), lambda i, h: (0, 0))
    out_blk = pl.BlockSpec((rows, HEAD_DIM), lambda i, h: (i, h))
    f32_out = jax.ShapeDtypeStruct((n, cols), F32)
    bf_out = jax.ShapeDtypeStruct((n, cols), BF16)
    eg_out = jax.ShapeDtypeStruct((n // SUBLANES, cols), F32)
    eg_blk = pl.BlockSpec((rows // SUBLANES, HEAD_DIM), lambda i, h: (i, h))
    tile = pltpu.VMEM((rows, LANES), F32)
    tposed = pltpu.VMEM((rows // LANES, LANES, LANES), F32)
    return pl.pallas_call(
        _delta_local_body,
        grid=(n // rows, N_HEADS),
        in_specs=[blk(0), blk(N_HEADS), blk(2 * N_HEADS),
                  pl.BlockSpec((rows, LANES), lambda i, h: (i, 0)), row_spec, row_spec],
        out_specs=[out_blk, out_blk, out_blk, out_blk, out_blk, out_blk, out_blk,
                   pl.BlockSpec((2 * rows, HEAD_DIM), lambda i, h: (i, h)), eg_blk, eg_blk],
        out_shape=[f32_out, f32_out, bf_out, bf_out, bf_out, bf_out, bf_out,
                   jax.ShapeDtypeStruct((2 * n, cols), BF16), eg_out, eg_out],
        scratch_shapes=[tile, tile, tile, tposed, tposed],
        compiler_params=_params("parallel", "arbitrary"),
        name="delta_local",
    )(qkv, qkv, qkv, p_small, alog_row, dtb_row)


def _delta_scan_body(uf_ref, wf_ref, qf_ref, af_ref, ktf_ref, ef_ref,
                     ub_ref, wb_ref, qb_ref, ab_ref, ktb_ref, eb_ref,
                     of_ref, ob_ref, s_scr):
    @pl.when(pl.program_id(1) == 0)
    def _():
        s_scr[...] = jnp.zeros_like(s_scr)

    heads = [slice(h * HEAD_DIM, (h + 1) * HEAD_DIM) for h in range(N_HEADS)]

    def per_head(ref_f, ref_b, rows=slice(None), half=False):
        lo, hi = (slice(0, CHUNK), slice(CHUNK, 2 * CHUNK)) if half else (slice(None), slice(None))
        return jnp.stack([ref_f[rows, hs][:, lo] for hs in heads]
                         + [ref_b[rows, hs][:, hi] for hs in heads])

    state = s_scr[...]
    wq = jnp.concatenate([per_head(wf_ref, wb_ref), per_head(qf_ref, qb_ref)], axis=1)
    x = _bmm(wq, state.astype(BF16))
    v_new = (per_head(uf_ref, ub_ref) - x[:, :CHUNK]).astype(BF16)
    lhs = jnp.concatenate([per_head(af_ref, ab_ref, half=True),
                           per_head(ktf_ref, ktb_ref, half=True)], axis=1)
    y = _bmm(lhs, v_new)
    out = x[:, CHUNK:] + y[:, :CHUNK]
    for h, hs in enumerate(heads):
        of_ref[:, hs] = out[h]
        ob_ref[:, hs] = out[N_HEADS + h]
    s_scr[...] = state * per_head(ef_ref, eb_ref, rows=slice(0, 1)) + y[:, CHUNK:]


def _delta_scan(local_out, bsz, seq):
    uf, ub, wf, wb, qf, qb, a, kt, ef, eb = local_out
    n, cols = uf.shape
    n_chunks = seq // CHUNK
    fwd = lambda b, c: (b * n_chunks + c, 0)
    bwd = lambda b, c: (b * n_chunks + (n_chunks - 1 - c), 0)
    specs = lambda idx: [pl.BlockSpec((CHUNK, cols), idx), pl.BlockSpec((CHUNK, cols), idx),
                         pl.BlockSpec((CHUNK, cols), idx), pl.BlockSpec((CHUNK, cols), idx),
                         pl.BlockSpec((2 * CHUNK, cols), idx), pl.BlockSpec((SUBLANES, cols), idx)]
    return pl.pallas_call(
        _delta_scan_body,
        grid=(bsz, n_chunks),
        in_specs=specs(fwd) + specs(bwd),
        out_specs=[pl.BlockSpec((CHUNK, cols), fwd), pl.BlockSpec((CHUNK, cols), bwd)],
        out_shape=[jax.ShapeDtypeStruct((n, cols), F32), jax.ShapeDtypeStruct((n, cols), F32)],
        scratch_shapes=[pltpu.VMEM((2 * N_HEADS, HEAD_DIM, HEAD_DIM), F32)],
        compiler_params=_params("parallel", "arbitrary"),
        name="delta_scan",
    )(uf, wf, qf, a, kt, ef, ub, wb, qb, a, kt, eb)


def _mixer_out_body(of_ref, ob_ref, z_ref, yb_ref, ga_ref, gb_ref, x_ref,
                    onw_ref, wa_ref, wb_ref, wo_ref, gtm_ref, n2w_ref, scf_ref, shf_ref,
                    rw_ref, rb_ref, x1_ref, h2_ref, lg_ref, oz_scr):
    o = of_ref[...] + ob_ref[...]
    z = z_ref[...]
    onw = onw_ref[...]
    for h in range(N_HEADS):
        hs = slice(h * HEAD_DIM, (h + 1) * HEAD_DIM)
        oh = o[:, hs]
        ms = jnp.mean(oh * oh, axis=-1, keepdims=True)
        zh = z[:, hs]
        oz_scr[:, hs] = (oh * lax.rsqrt(ms + EPS) * onw * (zh * _sigmoid(zh))).astype(BF16)
    y_a = _dot(oz_scr[...], wa_ref[...])
    y_b = _dot(yb_ref[...].astype(BF16), wb_ref[...])
    merged = _sigmoid(ga_ref[...]) * y_a + _sigmoid(gb_ref[...]) * y_b
    x1 = x_ref[...] + gtm_ref[0] * _dot(merged.astype(BF16), wo_ref[...])
    x1_ref[...] = x1
    ms = jnp.mean(x1 * x1, axis=-1, keepdims=True)
    h2 = x1 * lax.rsqrt(ms + EPS) * n2w_ref[...] * (1.0 + scf_ref[0]) + shf_ref[0]
    _rows_to_tiles(h2_ref, h2)
    lg_ref[...] = jnp.dot(h2, rw_ref[...], preferred_element_type=F32,
                          precision=lax.Precision.HIGHEST) + rb_ref[...]


def _mixer_out(o_f, o_b, p_main, y_sc, x2d, onorm_w, w_up_a, w_out_sc, w_o, gt_m, norm2_w,
               sc_f, sh_f, router_w, router_b, seq, z_blk, ga_blk):
    n, d = x2d.shape
    tm = min(256, seq)
    tiles_per_batch = seq // tm
    row = lambda i: (i, 0)
    const = lambda i: (0, 0)
    per_b = lambda i: (i // tiles_per_batch, 0, 0)
    rw = jnp.pad(router_w, ((0, 0), (0, LANES - N_EXPERTS)))
    rb = jnp.pad(router_b.reshape(1, -1), ((0, 0), (0, LANES - N_EXPERTS)))
    return pl.pallas_call(
        _mixer_out_body,
        grid=(n // tm,),
        in_specs=[pl.BlockSpec((tm, d), row), pl.BlockSpec((tm, d), row),
                  pl.BlockSpec((tm, d), lambda i: (i, z_blk)), pl.BlockSpec((tm, d), row),
                  pl.BlockSpec((tm, d), lambda i: (i, ga_blk)),
                  pl.BlockSpec((tm, d), lambda i: (i, ga_blk + 1)), pl.BlockSpec((tm, d), row),
                  pl.BlockSpec((1, HEAD_DIM), const), pl.BlockSpec((d, d), const),
                  pl.BlockSpec((d, d), const), pl.BlockSpec((d, d), const),
                  pl.BlockSpec((1, 1, d), per_b), pl.BlockSpec((1, d), const),
                  pl.BlockSpec((1, 1, d), per_b), pl.BlockSpec((1, 1, d), per_b),
                  pl.BlockSpec((d, LANES), const), pl.BlockSpec((1, LANES), const)],
        out_specs=[pl.BlockSpec((tm, d), row), pl.BlockSpec((tm * SUBLANES, LANES), row),
                   pl.BlockSpec((tm, LANES), row)],
        out_shape=[jax.ShapeDtypeStruct((n, d), F32),
                   jax.ShapeDtypeStruct((n * SUBLANES, LANES), F32),
                   jax.ShapeDtypeStruct((n, LANES), F32)],
        scratch_shapes=[pltpu.VMEM((tm, d), BF16)],
        compiler_params=_params("parallel"),
        name="mixer_out",
    )(o_f, o_b, p_main, y_sc, p_main, p_main, x2d, onorm_w.reshape(1, HEAD_DIM),
      w_up_a.astype(BF16), w_out_sc.astype(BF16), w_o.astype(BF16), gt_m,
      norm2_w.reshape(1, d), sc_f, sh_f, rw, rb)


def _route_body(lg_ref, er_ref, gate_ref, cnt_ref, carry_scr):
    @pl.when(pl.program_id(0) == 0)
    def _():
        carry_scr[...] = jnp.zeros_like(carry_scr)

    rows = lg_ref.shape[0]
    lane = lax.broadcasted_iota(jnp.int32, (rows, LANES), 1)
    lanef = lane.astype(F32)
    logits = jnp.where(lane < N_EXPERTS, lg_ref[...], -jnp.inf)
    vals, idxs = [], []
    sel = jnp.zeros((rows, LANES), F32)
    for _ in range(TOP_K):
        m = jnp.max(logits, axis=-1, keepdims=True)
        idx = jnp.min(jnp.where(logits == m, lanef, float(LANES)), axis=-1, keepdims=True)
        hit = lanef == idx
        vals.append(m)
        idxs.append(idx)
        sel = sel + hit.astype(F32)
        logits = jnp.where(hit, -jnp.inf, logits)
    exps = [jnp.exp(v - vals[0]) for v in vals]
    total = exps[0]
    for e in exps[1:]:
        total = total + e
    r = lax.broadcasted_iota(jnp.int32, (rows, rows), 0)
    c = lax.broadcasted_iota(jnp.int32, (rows, rows), 1)
    earlier = (r > c).astype(BF16)
    carry = carry_scr[0:1, :]
    before = _dot(earlier, sel.astype(BF16)) + carry
    er = jnp.zeros((rows, LANES), F32)
    gates = jnp.zeros((rows, LANES), F32)
    for k in range(TOP_K):
        rank = jnp.sum(jnp.where(lanef == idxs[k], before, 0.0), axis=-1, keepdims=True)
        er = jnp.where(lane == k, idxs[k], er)
        er = jnp.where(lane == TOP_K + k, rank, er)
        gates = jnp.where(lane == k, exps[k] / total, gates)
    er_ref[...] = er
    gate_ref[...] = gates
    new_carry = carry + jnp.sum(sel, axis=0, keepdims=True)
    carry_scr[...] = jnp.broadcast_to(new_carry, carry_scr.shape)
    cnt_ref[...] = jnp.broadcast_to(new_carry, cnt_ref.shape)


def _route(logits):
    n = logits.shape[0]
    rows = min(256, n)
    tile = pl.BlockSpec((rows, LANES), lambda i: (i, 0))
    return pl.pallas_call(
        _route_body,
        grid=(n // rows,),
        in_specs=[tile],
        out_specs=[tile, tile, pl.BlockSpec((SUBLANES, LANES), lambda i: (0, 0))],
        out_shape=[jax.ShapeDtypeStruct((n, LANES), F32), jax.ShapeDtypeStruct((n, LANES), F32),
                   jax.ShapeDtypeStruct((SUBLANES, LANES), F32)],
        scratch_shapes=[pltpu.VMEM((SUBLANES, LANES), F32)],
        compiler_params=_params("arbitrary"),
        name="route",
    )(logits)


def _dest_body(er_ref, ps_ref, o_ref):
    rows = er_ref.shape[0]
    lane = lax.broadcasted_iota(jnp.int32, (rows, LANES), 1)
    lanef = lane.astype(F32)
    er = er_ref[...]
    starts = ps_ref[0:1, :]
    out = jnp.zeros((rows, LANES), F32)
    for k in range(TOP_K):
        e_k = jnp.sum(jnp.where(lane == k, er, 0.0), axis=-1, keepdims=True)
        r_k = jnp.sum(jnp.where(lane == TOP_K + k, er, 0.0), axis=-1, keepdims=True)
        base = jnp.sum(jnp.where(lanef == e_k, starts, 0.0), axis=-1, keepdims=True)
        out = jnp.where(lane == k, base + r_k, out)
    o_ref[...] = out.astype(jnp.int32)


def _dest_rows(er, pad_start_row):
    n = er.shape[0]
    rows = min(1024, n)
    tile = pl.BlockSpec((rows, LANES), lambda i: (i, 0))
    return pl.pallas_call(
        _dest_body,
        grid=(n // rows,),
        in_specs=[tile, pl.BlockSpec((SUBLANES, LANES), lambda i: (0, 0))],
        out_specs=tile,
        out_shape=jax.ShapeDtypeStruct((n, LANES), jnp.int32),
        compiler_params=_params("parallel"),
        name="dest_rows",
    )(er, pad_start_row)


def _tile_at(ref, row):
    return ref.at[pl.ds(pl.multiple_of(row * SUBLANES, SUBLANES), SUBLANES)]


def _dispatch_body(dest_ref, h_ref, xs_init_hbm, xs_hbm, sem):
    del xs_init_hbm
    n_tok = dest_ref.shape[-1] // TOP_K

    def issue(t, carry):
        for k in range(TOP_K):
            pltpu.make_async_copy(_tile_at(h_ref, t), _tile_at(xs_hbm, dest_ref[0, 0, t * TOP_K + k]),
                                  sem).start(priority=k % 2)
        return carry

    def drain(t, carry):
        for _ in range(TOP_K):
            pltpu.make_async_copy(_tile_at(h_ref, 0), _tile_at(xs_hbm, 0), sem).wait()
        return carry

    lax.fori_loop(0, n_tok, issue, 0)
    lax.fori_loop(0, n_tok, drain, 0)


def _dispatch(dest_flat, h2_tiles, n_tok, m_pad):
    rows = min(256, n_tok)
    dest3 = dest_flat.reshape(n_tok // rows, 1, rows * TOP_K)
    xs_init = jnp.zeros((m_pad * SUBLANES, LANES), h2_tiles.dtype)
    return pl.pallas_call(
        _dispatch_body,
        grid=(n_tok // rows,),
        in_specs=[pl.BlockSpec((1, 1, rows * TOP_K), lambda i: (i, 0, 0),
                               memory_space=pltpu.SMEM),
                  pl.BlockSpec((rows * SUBLANES, LANES), lambda i: (i, 0)),
                  pl.BlockSpec(memory_space=pl.ANY)],
        out_specs=pl.BlockSpec(memory_space=pl.ANY),
        out_shape=jax.ShapeDtypeStruct((m_pad * SUBLANES, LANES), h2_tiles.dtype),
        scratch_shapes=[pltpu.SemaphoreType.DMA],
        input_output_aliases={2: 0},
        compiler_params=_params("arbitrary"),
        name="dispatch",
    )(dest3, h2_tiles, xs_init)


def _expert_body(be_ref, nu_ref, x_ref, w1_ref, b1_ref, w2_ref, b2_ref, o_ref):
    del be_ref
    i = pl.program_id(0)

    @pl.when(i < nu_ref[0])
    def _():
        d_ff = w2_ref.shape[1]
        x = _tiles_to_rows(x_ref, MOE_ROWS)
        gu = _dot(x.astype(BF16), w1_ref[0]) + b1_ref[0]
        gate = jnp.minimum(gu[:, :d_ff], SWIGLU_LIMIT)
        up = jnp.clip(gu[:, d_ff:], -SWIGLU_LIMIT, SWIGLU_LIMIT)
        act = gate * _sigmoid(SWIGLU_ALPHA * gate) * (up + 1.0)
        _rows_to_tiles(o_ref, _dot(act.astype(BF16), w2_ref[0]) + b2_ref[0])

    @pl.when(i >= nu_ref[0])
    def _():
        o_ref[...] = jnp.zeros_like(o_ref)


def _experts(xs_tiles, blk_exp, n_used, w1, b1, w2, b2):
    n_exp, d, two_ff = w1.shape
    d_ff = two_ff // 2
    m_pad = xs_tiles.shape[0] // SUBLANES
    tiles = pl.BlockSpec((MOE_ROWS * SUBLANES, LANES), lambda i, be, nu: (i, 0))
    grid_spec = pltpu.PrefetchScalarGridSpec(
        num_scalar_prefetch=2,
        grid=(m_pad // MOE_ROWS,),
        in_specs=[tiles,
                  pl.BlockSpec((1, d, two_ff), lambda i, be, nu: (be[i], 0, 0)),
                  pl.BlockSpec((1, 1, two_ff), lambda i, be, nu: (be[i], 0, 0)),
                  pl.BlockSpec((1, d_ff, d), lambda i, be, nu: (be[i], 0, 0)),
                  pl.BlockSpec((1, 1, d), lambda i, be, nu: (be[i], 0, 0))],
        out_specs=tiles,
    )
    return pl.pallas_call(
        _expert_body,
        grid_spec=grid_spec,
        out_shape=jax.ShapeDtypeStruct((m_pad * SUBLANES, LANES), F32),
        compiler_params=_params("arbitrary"),
        name="experts",
    )(blk_exp, n_used, xs_tiles, w1.astype(BF16), b1.reshape(n_exp, 1, two_ff), w2.astype(BF16),
      b2.reshape(n_exp, 1, d))


def _combine_body(dest_ref, y_hbm, x1_ref, gate_ref, gtf_ref, fw_ref, o_ref, buf, sem):
    rows = x1_ref.shape[0]

    def issue(t, carry):
        for k in range(TOP_K):
            pltpu.make_async_copy(_tile_at(y_hbm, dest_ref[0, 0, t * TOP_K + k]),
                                  _tile_at(buf.at[k], t), sem).start(priority=k % 2)
        return carry

    def drain(t, carry):
        for k in range(TOP_K):
            pltpu.make_async_copy(_tile_at(y_hbm, 0), _tile_at(buf.at[k], 0), sem).wait()
        return carry

    lax.fori_loop(0, rows, issue, 0)
    lax.fori_loop(0, rows, drain, 0)
    gates = gate_ref[...]
    moe = gates[:, 0:1] * _tiles_to_rows(buf.at[0], rows)
    for k in range(1, TOP_K):
        moe = moe + gates[:, k:k + 1] * _tiles_to_rows(buf.at[k], rows)
    x2 = x1_ref[...] + gtf_ref[0] * moe
    ms = jnp.mean(x2 * x2, axis=-1, keepdims=True)
    o_ref[...] = x2 * lax.rsqrt(ms + EPS) * fw_ref[...]


def _combine(dest_flat, yb_tiles, x1, gates, gt_f, final_w, seq):
    n, d = x1.shape
    rows = min(128, seq)
    tiles_per_batch = seq // rows
    dest3 = dest_flat.reshape(n // rows, 1, rows * TOP_K)
    return pl.pallas_call(
        _combine_body,
        grid=(n // rows,),
        in_specs=[pl.BlockSpec((1, 1, rows * TOP_K), lambda i: (i, 0, 0),
                               memory_space=pltpu.SMEM),
                  pl.BlockSpec(memory_space=pl.ANY),
                  pl.BlockSpec((rows, d), lambda i: (i, 0)),
                  pl.BlockSpec((rows, LANES), lambda i: (i, 0)),
                  pl.BlockSpec((1, 1, d), lambda i: (i // tiles_per_batch, 0, 0)),
                  pl.BlockSpec((1, d), lambda i: (0, 0))],
        out_specs=pl.BlockSpec((rows, d), lambda i: (i, 0)),
        out_shape=jax.ShapeDtypeStruct((n, d), F32),
        scratch_shapes=[pltpu.VMEM((TOP_K, rows * SUBLANES, LANES), F32), pltpu.SemaphoreType.DMA],
        compiler_params=_params("arbitrary"),
        name="combine",
    )(dest3, yb_tiles, x1, gates, gt_f, final_w.reshape(1, d))


def _layer(x2d, c_mod, bsz, seq, norm1_w, w_in, conv_qkv_w, a_log, dt_bias, onorm_w, w_up_a,
           conv_sc_w, w_out_sc, w_o, norm2_w, router_w, router_b, moe_w1, moe_b1, moe_w2, moe_b2):
    n, d = x2d.shape
    dn = N_HEADS * HEAD_DIM
    sh_m, sc_m, gt_m, sh_f, sc_f, gt_f = (m.reshape(bsz, 1, d) for m in jnp.split(c_mod, 6, axis=-1))

    o_small, o_sc = 4 * dn, 4 * dn + 4 * N_HEADS
    w_main = jnp.concatenate([w_in[:, :o_small], w_in[:, o_sc:]], axis=1).astype(BF16)
    w_small = jnp.pad(w_in[:, o_small:o_sc], ((0, 0), (0, LANES - 4 * N_HEADS))).astype(BF16)
    p_main, p_small = _in_projection(x2d, norm1_w, sc_m, sh_m, w_main, w_small, seq)

    qkv = _qkv_conv(p_main, conv_qkv_w, seq)
    y_sc = _sc_conv(p_main, conv_sc_w, seq, first_col_block=4 * dn // d)
    local = _delta_local(qkv, p_small, a_log, dt_bias, seq)
    o_f, o_b = _delta_scan(local, bsz, seq)
    x1, h2_tiles, logits = _mixer_out(o_f, o_b, p_main, y_sc, x2d, onorm_w, w_up_a, w_out_sc, w_o,
                                      gt_m, norm2_w, sc_f, sh_f, router_w, router_b, seq,
                                      z_blk=3 * dn // d, ga_blk=(4 * dn + 3 * d) // d)

    er, gates, counts = _route(logits)
    m_rows = n * TOP_K
    n_blocks = m_rows // MOE_ROWS + N_EXPERTS
    cnt = counts[0, :N_EXPERTS].astype(jnp.int32)
    padded = (cnt + MOE_ROWS - 1) // MOE_ROWS * MOE_ROWS
    pad_end = jnp.cumsum(padded)
    pad_start = pad_end - padded
    blk_first_row = jnp.arange(n_blocks, dtype=jnp.int32) * MOE_ROWS
    blk_exp = jnp.minimum(jnp.sum(pad_end[None, :] <= blk_first_row[:, None], axis=1),
                          N_EXPERTS - 1).astype(jnp.int32)
    n_used = (pad_end[-1:] // MOE_ROWS).astype(jnp.int32)
    ps_row = jnp.broadcast_to(
        jnp.pad(pad_start.astype(F32), (0, LANES - N_EXPERTS)).reshape(1, LANES), (SUBLANES, LANES))
    dest = _dest_rows(er, ps_row)[:, :TOP_K].reshape(m_rows)

    xs_tiles = _dispatch(dest, h2_tiles, n, n_blocks * MOE_ROWS)
    yb_tiles = _experts(xs_tiles, blk_exp, n_used, moe_w1, moe_b1, moe_w2, moe_b2)
    return dest, yb_tiles, x1, gates, gt_f


def kernel(x, c, ada_w, ada_b, norm1_w, w_in, conv_qkv_w, a_log, dt_bias, onorm_w, w_up_a, conv_sc_w, w_out_sc, w_o, norm2_w, router_w, router_b, moe_w1, moe_b1, moe_w2, moe_b2, final_norm_w):
    bsz, seq, d = x.shape
    depth = ada_w.shape[0]
    assert depth == 1, "the combine stage fuses the final norm, which needs a single layer"
    x2d = x.reshape(bsz * seq, d)
    c_mod = _ada_mod(c, ada_w[0], ada_b[0])
    dest, yb_tiles, x1, gates, gt_f = _layer(
        x2d, c_mod, bsz, seq, norm1_w[0], w_in[0], conv_qkv_w[0], a_log[0], dt_bias[0], onorm_w[0],
        w_up_a[0], conv_sc_w[0], w_out_sc[0], w_o[0], norm2_w[0], router_w[0], router_b[0],
        moe_w1[0], moe_b1[0], moe_w2[0], moe_b2[0])
    out = _combine(dest, yb_tiles, x1, gates, gt_f, final_norm_w, seq)
    return out.reshape(bsz, seq, d)
```

```python
import functools

import jax
import jax.numpy as jnp
from jax import lax
from jax.experimental import pallas as pl
from jax.experimental.pallas import tpu as pltpu

F32 = jnp.float32
BF16 = jnp.bfloat16

EPS = 1e-6
N_HEADS = 8
HEAD_DIM = 128
CHUNK = 64
N_EXPERTS = 32
TOP_K = 4
SWIGLU_LIMIT = 7.0
SWIGLU_ALPHA = 1.702
LANES = 128
SUBLANES = 8
MOE_ROWS = 512
SCAN_CHUNKS = 4
VMEM_LIMIT = 56 * 1024 * 1024


def _sigmoid(x):
    return 1.0 / (1.0 + jnp.exp(-x))


def _softplus(x):
    return jnp.maximum(x, 0.0) + jnp.log(1.0 + jnp.exp(-jnp.abs(x)))


def _dot(a, b):
    return jnp.dot(a, b, preferred_element_type=F32)


def _bmm(a, b):
    return lax.dot_general(a, b, (((2,), (1,)), ((0,), (0,))), preferred_element_type=F32)


def _bmm_nt(a, b):
    return lax.dot_general(a, b, (((2,), (2,)), ((0,), (0,))), preferred_element_type=F32)


def _params(*sem, vmem=VMEM_LIMIT):
    return pltpu.CompilerParams(dimension_semantics=sem, vmem_limit_bytes=vmem)


def _rows_to_tiles(ref, value):
    n_rows = value.shape[0]
    for s in range(SUBLANES):
        ref[pl.ds(s, n_rows, stride=SUBLANES), :] = value[:, s * LANES:(s + 1) * LANES]


def _tiles_to_rows(ref, n_rows):
    return jnp.concatenate(
        [ref[pl.ds(s, n_rows, stride=SUBLANES), :] for s in range(SUBLANES)], axis=1)


def _ada_body(c_ref, w_ref, b_ref, o_ref):
    c = c_ref[...]
    ca = c * _sigmoid(c)
    o_ref[...] = _dot(ca.astype(BF16), w_ref[...].astype(BF16)) + b_ref[...]


def _ada_mod(c, ada_w, ada_b):
    bsz, d = c.shape
    n_out = ada_w.shape[1]
    cp = jnp.pad(c, ((0, SUBLANES - bsz), (0, 0)))
    out = pl.pallas_call(
        _ada_body,
        grid=(n_out // d,),
        in_specs=[pl.BlockSpec((SUBLANES, d), lambda j: (0, 0)),
                  pl.BlockSpec((d, d), lambda j: (0, j)),
                  pl.BlockSpec((1, d), lambda j: (0, j))],
        out_specs=pl.BlockSpec((SUBLANES, d), lambda j: (0, j)),
        out_shape=jax.ShapeDtypeStruct((SUBLANES, n_out), F32),
        compiler_params=_params("parallel"),
        name="ada_mod",
    )(cp, ada_w, ada_b.reshape(1, n_out))
    return out[:bsz]


def _inproj_body(x_ref, nw_ref, sc_ref, sh_ref, w_ref, ws_ref, o_ref, os_ref, h_scr):
    @pl.when(pl.program_id(1) == 0)
    def _():
        x = x_ref[...]
        ms = jnp.mean(x * x, axis=-1, keepdims=True)
        xn = x * lax.rsqrt(ms + EPS) * nw_ref[...]
        h = (xn * (1.0 + sc_ref[0]) + sh_ref[0]).astype(BF16)
        h_scr[...] = h
        os_ref[...] = _dot(h, ws_ref[...])

    o_ref[...] = _dot(h_scr[...], w_ref[...])


def _in_projection(x2d, norm_w, sc_m, sh_m, w_main, w_small, seq):
    n, d = x2d.shape
    width = w_main.shape[1]
    tm = min(1024, seq)
    tn = 1024
    tiles_per_batch = seq // tm
    return pl.pallas_call(
        _inproj_body,
        grid=(n // tm, width // tn),
        in_specs=[pl.BlockSpec((tm, d), lambda i, j: (i, 0)),
                  pl.BlockSpec((1, d), lambda i, j: (0, 0)),
                  pl.BlockSpec((1, 1, d), lambda i, j: (i // tiles_per_batch, 0, 0)),
                  pl.BlockSpec((1, 1, d), lambda i, j: (i // tiles_per_batch, 0, 0)),
                  pl.BlockSpec((d, tn), lambda i, j: (0, j)),
                  pl.BlockSpec((d, LANES), lambda i, j: (0, 0))],
        out_specs=[pl.BlockSpec((tm, tn), lambda i, j: (i, j)),
                   pl.BlockSpec((tm, LANES), lambda i, j: (i, 0))],
        out_shape=[jax.ShapeDtypeStruct((n, width), F32),
                   jax.ShapeDtypeStruct((n, LANES), F32)],
        scratch_shapes=[pltpu.VMEM((tm, d), BF16)],
        compiler_params=_params("parallel", "arbitrary"),
        name="in_proj",
    )(x2d, norm_w.reshape(1, d), sc_m, sh_m, w_main, w_small)


def _conv_staged(scr, prev, cur, nxt, cw, width):
    rows = cur.shape[0]
    scr[0:SUBLANES, :] = prev
    scr[SUBLANES:SUBLANES + rows, :] = cur
    scr[SUBLANES + rows:, :] = nxt
    pad = width // 2
    acc = scr[pl.ds(SUBLANES - pad, rows), :] * cw[0:1]
    for w in range(1, width):
        acc = acc + scr[pl.ds(SUBLANES - pad + w, rows), :] * cw[w:w + 1]
    return acc


def _halo_specs(rows, cols, col_block, n_rows):
    per = rows // SUBLANES
    last = n_rows // SUBLANES - 1
    cur = pl.BlockSpec((rows, cols), lambda i, j: (i, col_block(j)))
    prev = pl.BlockSpec((SUBLANES, cols), lambda i, j: (jnp.maximum(i * per - 1, 0), col_block(j)))
    nxt = pl.BlockSpec((SUBLANES, cols), lambda i, j: (jnp.minimum((i + 1) * per, last), col_block(j)))
    return cur, prev, nxt


def _sc_conv_body(b_ref, c_ref, cp_ref, cn_ref, u_ref, up_ref, un_ref, cw_ref, o_ref, *,
                  tiles_per_batch, width):
    t_in_b = pl.program_id(0) % tiles_per_batch
    rows = c_ref.shape[0]
    length = rows + 2 * SUBLANES
    prev = jnp.where(t_in_b == 0, 0.0, cp_ref[...] * up_ref[...])
    nxt = jnp.where(t_in_b == tiles_per_batch - 1, 0.0, cn_ref[...] * un_ref[...])
    ext = jnp.concatenate([prev, c_ref[...] * u_ref[...], nxt], axis=0)
    cw = cw_ref[...]
    pad = width // 2
    acc = ext * cw[pad:pad + 1]
    for w in range(width):
        if w != pad:
            acc = acc + pltpu.roll(ext, (pad - w) % length, axis=0) * cw[w:w + 1]
    o_ref[...] = b_ref[...] * acc[SUBLANES:SUBLANES + rows]


def _sc_conv(p_main, conv_w, seq, first_col_block):
    n = p_main.shape[0]
    width, ch = conv_w.shape
    rows = min(512, seq)
    cw = jnp.pad(conv_w, ((0, SUBLANES - width), (0, 0)))
    b_spec = pl.BlockSpec((rows, ch), lambda i, j: (i, first_col_block))
    c_specs = _halo_specs(rows, ch, lambda j: first_col_block + 1, n)
    u_specs = _halo_specs(rows, ch, lambda j: first_col_block + 2, n)
    body = functools.partial(_sc_conv_body, tiles_per_batch=seq // rows, width=width)
    return pl.pallas_call(
        body,
        grid=(n // rows, 1),
        in_specs=[b_spec, *c_specs, *u_specs, pl.BlockSpec((SUBLANES, ch), lambda i, j: (0, 0))],
        out_specs=pl.BlockSpec((rows, ch), lambda i, j: (i, 0)),
        out_shape=jax.ShapeDtypeStruct((n, ch), F32),
        compiler_params=_params("parallel", "arbitrary"),
        name="sc_conv",
    )(p_main, p_main, p_main, p_main, p_main, p_main, p_main, cw)


def _block_diag(x, lo_half):
    return jnp.concatenate([jnp.where(lo_half, x, 0.0), jnp.where(lo_half, 0.0, x)], axis=1)


def _delta_local_body(qc_ref, qp_ref, qn_ref, qw_ref, kc_ref, kp_ref, kn_ref, kw_ref,
                      vc_ref, vp_ref, vn_ref, vw_ref, s_ref, alog_ref, dtb_ref,
                      uf_ref, ub_ref, wf_ref, wb_ref, qf_ref, qb_ref, a_ref, kt_ref, ef_ref, eb_ref,
                      beta_scr, g_scr, gl_scr, gt_scr, gtr_scr, cq_scr, ck_scr, cv_scr, *,
                      tiles_per_batch, width):
    h = pl.program_id(1)
    rows = qc_ref.shape[0]
    n_chunks = rows // CHUNK
    lane = lax.broadcasted_iota(jnp.int32, (rows, LANES), 1)

    @pl.when(h == 0)
    def _():
        s = s_ref[...]
        beta_scr[...] = _sigmoid(s)
        g = -jnp.exp(alog_ref[...]) * _softplus(s + dtb_ref[...])
        pos = lax.broadcasted_iota(jnp.int32, (rows, LANES), 0) & (CHUNK - 1)
        gp = g
        gs = g
        step = 1
        while step < CHUNK:
            gp = gp + jnp.where(pos >= step, pltpu.roll(gp, step, axis=0), 0.0)
            gs = gs + jnp.where(pos < CHUNK - step, pltpu.roll(gs, rows - step, axis=0), 0.0)
            step *= 2
        gc = jnp.where(lane >= 3 * N_HEADS, gs, gp)
        g_scr[...] = gc
        gl_scr[...] = gp + gs - g
        for c2 in range(rows // LANES):
            gt = gc[c2 * LANES:(c2 + 1) * LANES, :].T
            gt_scr[c2] = gt
            gtr_scr[c2] = pltpu.roll(gt, CHUNK, axis=1)

    t_in_b = pl.program_id(0) % tiles_per_batch

    def conv_in(cur_ref, prev_ref, next_ref, w_ref, scr, normed):
        prev = jnp.where(t_in_b == 0, 0.0, prev_ref[...])
        nxt = jnp.where(t_in_b == tiles_per_batch - 1, 0.0, next_ref[...])
        y = _conv_staged(scr, prev, cur_ref[...], nxt, w_ref[...], width)
        y = y * _sigmoid(y)
        if normed:
            y = y * lax.rsqrt(jnp.sum(y * y, axis=-1, keepdims=True) + EPS)
        return y

    q2 = conv_in(qc_ref, qp_ref, qn_ref, qw_ref, cq_scr, True) * (HEAD_DIM ** -0.5)
    k2 = conv_in(kc_ref, kp_ref, kn_ref, kw_ref, ck_scr, True)
    v2 = conv_in(vc_ref, vp_ref, vn_ref, vw_ref, cv_scr, False)

    def pick(ref, idx):
        return jnp.sum(jnp.where(lane == idx, ref[...], 0.0), axis=-1, keepdims=True)

    i_f, i_b = 2 * N_HEADS + h, 3 * N_HEADS + h
    beta_f, beta_b = pick(beta_scr, h), pick(beta_scr, N_HEADS + h)
    g_f, g_b = pick(g_scr, i_f), pick(g_scr, i_b)
    gl_f, gl_b = pick(gl_scr, i_f), pick(gl_scr, i_b)

    kb_f, kb_b = k2 * beta_f, k2 * beta_b
    eg_f, eg_b = jnp.exp(g_f), jnp.exp(g_b)
    to3 = lambda a: a.reshape(n_chunks, CHUNK, a.shape[-1])
    bf3 = lambda a: to3(a.astype(BF16))

    k3 = bf3(k2)
    zero3 = jnp.zeros_like(k3)
    lhs = jnp.concatenate([jnp.concatenate([bf3(kb_f), bf3(kb_b)], axis=2),
                           jnp.concatenate([bf3(q2), bf3(q2)], axis=2)], axis=1)
    rhs_t = jnp.concatenate([jnp.concatenate([k3, zero3], axis=2),
                             jnp.concatenate([zero3, k3], axis=2)], axis=1)
    m1 = _bmm_nt(lhs, rhs_t)

    ii = lax.broadcasted_iota(jnp.int32, (CHUNK, LANES), 0)
    l2 = lax.broadcasted_iota(jnp.int32, (CHUNK, LANES), 1)
    jj = l2 & (CHUNK - 1)
    lo_half = l2 < CHUNK
    ahead = jnp.where(lo_half, ii - jj, jj - ii)
    incl = ahead >= 0
    strict = ahead > 0
    eye = (ii == jj).astype(F32)

    g_cols = jnp.where(lane < CHUNK, g_f, g_b)
    lrow = lax.broadcasted_iota(jnp.int32, (1, LANES), 1) < CHUNK
    diffs = []
    for c in range(n_chunks):
        rf = gt_scr[c // 2, pl.ds(i_f, 1), :]
        rb = gt_scr[c // 2, pl.ds(i_b, 1), :]
        if c % 2 == 0:
            g_row = jnp.where(lrow, rf, gtr_scr[c // 2, pl.ds(i_b, 1), :])
        else:
            g_row = jnp.where(lrow, gtr_scr[c // 2, pl.ds(i_f, 1), :], rb)
        diffs.append(g_cols[c * CHUNK:(c + 1) * CHUNK] - g_row)
    decay = jnp.exp(jnp.where(incl, jnp.stack(diffs), -jnp.inf))

    nm = -jnp.where(strict, m1[:, :CHUNK] * decay, 0.0)
    a_ref[...] = (m1[:, CHUNK:] * decay).astype(BF16).reshape(rows, LANES)

    p = eye + nm
    npow = _bmm(nm.astype(BF16), _block_diag(nm, lo_half).astype(BF16))
    n_iter = CHUNK.bit_length() - 2
    for it in range(n_iter):
        bd = _block_diag(npow, lo_half).astype(BF16)
        if it + 1 < n_iter:
            x = _bmm(jnp.concatenate([p, npow], axis=1).astype(BF16), bd)
            p = p + x[:, :CHUNK]
            npow = x[:, CHUNK:]
        else:
            p = p + _bmm(p.astype(BF16), bd)

    rhs = jnp.concatenate([jnp.concatenate([bf3(v2 * beta_f), bf3(kb_f * eg_f)], axis=2),
                           jnp.concatenate([bf3(v2 * beta_b), bf3(kb_b * eg_b)], axis=2)], axis=1)
    uw = _bmm(_block_diag(p, lo_half).astype(BF16), rhs)
    uf_ref[...] = uw[:, :CHUNK, :HEAD_DIM].reshape(rows, HEAD_DIM)
    ub_ref[...] = uw[:, CHUNK:, :HEAD_DIM].reshape(rows, HEAD_DIM)
    wf_ref[...] = uw[:, :CHUNK, HEAD_DIM:].astype(BF16).reshape(rows, HEAD_DIM)
    wb_ref[...] = uw[:, CHUNK:, HEAD_DIM:].astype(BF16).reshape(rows, HEAD_DIM)
    qf_ref[...] = (q2 * eg_f).astype(BF16)
    qb_ref[...] = (q2 * eg_b).astype(BF16)

    kd_f = to3(k2 * jnp.exp(gl_f - g_f))
    kd_b = to3(k2 * jnp.exp(gl_b - g_b))
    e_f = to3(jnp.broadcast_to(jnp.exp(gl_f), (rows, LANES)))
    e_b = to3(jnp.broadcast_to(jnp.exp(gl_b), (rows, LANES)))
    ef_ref[...] = e_f[:, :SUBLANES, :].reshape(n_chunks * SUBLANES, LANES)
    eb_ref[...] = e_b[:, :SUBLANES, :].reshape(n_chunks * SUBLANES, LANES)
    for c in range(n_chunks):
        kt_ref[c * 2 * CHUNK:(c + 1) * 2 * CHUNK, :] = (
            jnp.concatenate([kd_f[c], kd_b[c]], axis=0).T.astype(BF16))


def _delta_local(p_main, p_small, conv_w, a_log, dt_bias, seq):
    n = p_main.shape[0]
    width = conv_w.shape[0]
    cols = N_HEADS * HEAD_DIM
    rows = min(512, seq)
    cw = jnp.pad(conv_w, ((0, SUBLANES - width), (0, 0)))
    alog_row = jnp.pad(a_log.reshape(1, -1), ((0, 0), (2 * N_HEADS, LANES - 4 * N_HEADS)))
    dtb_row = jnp.pad(dt_bias.reshape(1, -1), ((0, 0), (2 * N_HEADS, LANES - 4 * N_HEADS)))

    def qkv_specs(off):
        col = lambda h: off + h
        return [*_halo_specs(rows, HEAD_DIM, col, n),
                pl.BlockSpec((SUBLANES, HEAD_DIM), lambda i, h: (0, off + h))]

    row_spec = pl.BlockSpec((1, LANES), lambda i, h: (0, 0))
    out_blk = pl.BlockSpec((rows, HEAD_DIM), lambda i, h: (i, h))
    f32_out = jax.ShapeDtypeStruct((n, cols), F32)
    bf_out = jax.ShapeDtypeStruct((n, cols), BF16)
    eg_out = jax.ShapeDtypeStruct((n // SUBLANES, cols), F32)
    eg_blk = pl.BlockSpec((rows // SUBLANES, HEAD_DIM), lambda i, h: (i, h))
    tile = pltpu.VMEM((rows, LANES), F32)
    tposed = pltpu.VMEM((rows // LANES, LANES, LANES), F32)
    staged = pltpu.VMEM((rows + 2 * SUBLANES, HEAD_DIM), F32)
    body = functools.partial(_delta_local_body, tiles_per_batch=seq // rows, width=width)
    qkv_in = [p_main, p_main, p_main, cw]
    return pl.pallas_call(
        body,
        grid=(n // rows, N_HEADS),
        in_specs=[*qkv_specs(0), *qkv_specs(N_HEADS), *qkv_specs(2 * N_HEADS),
                  pl.BlockSpec((rows, LANES), lambda i, h: (i, 0)), row_spec, row_spec],
        out_specs=[out_blk, out_blk, out_blk, out_blk, out_blk, out_blk, out_blk,
                   pl.BlockSpec((2 * rows, HEAD_DIM), lambda i, h: (i, h)), eg_blk, eg_blk],
        out_shape=[f32_out, f32_out, bf_out, bf_out, bf_out, bf_out, bf_out,
                   jax.ShapeDtypeStruct((2 * n, cols), BF16), eg_out, eg_out],
        scratch_shapes=[tile, tile, tile, tposed, tposed, staged, staged, staged],
        compiler_params=_params("parallel", "arbitrary"),
        name="delta_local",
    )(*qkv_in, *qkv_in, *qkv_in, p_small, alog_row, dtb_row)


def _delta_scan_body(uf_ref, wf_ref, qf_ref, af_ref, ktf_ref, ef_ref,
                     ub_ref, wb_ref, qb_ref, ab_ref, ktb_ref, eb_ref,
                     of_ref, ob_ref, s_scr, *, sub_chunks):
    @pl.when(pl.program_id(1) == 0)
    def _():
        s_scr[...] = jnp.zeros_like(s_scr)

    heads = [slice(h * HEAD_DIM, (h + 1) * HEAD_DIM) for h in range(N_HEADS)]
    state = s_scr[...]
    for j in range(sub_chunks):
        jb = sub_chunks - 1 - j

        def per_head(ref_f, ref_b, unit, rows=None, half=False):
            rows = unit if rows is None else rows
            lo, hi = ((slice(0, CHUNK), slice(CHUNK, 2 * CHUNK)) if half
                      else (slice(None), slice(None)))
            return jnp.stack([ref_f[j * unit:j * unit + rows, hs][:, lo] for hs in heads]
                             + [ref_b[jb * unit:jb * unit + rows, hs][:, hi] for hs in heads])

        wq = jnp.concatenate([per_head(wf_ref, wb_ref, CHUNK), per_head(qf_ref, qb_ref, CHUNK)],
                             axis=1)
        x = _bmm(wq, state.astype(BF16))
        v_new = (per_head(uf_ref, ub_ref, CHUNK) - x[:, :CHUNK]).astype(BF16)
        lhs = jnp.concatenate([per_head(af_ref, ab_ref, CHUNK, half=True),
                               per_head(ktf_ref, ktb_ref, 2 * CHUNK, half=True)], axis=1)
        y = _bmm(lhs, v_new)
        out = x[:, CHUNK:] + y[:, :CHUNK]
        for h, hs in enumerate(heads):
            of_ref[j * CHUNK:(j + 1) * CHUNK, hs] = out[h]
            ob_ref[jb * CHUNK:(jb + 1) * CHUNK, hs] = out[N_HEADS + h]
        state = state * per_head(ef_ref, eb_ref, SUBLANES, rows=1) + y[:, CHUNK:]
    s_scr[...] = state


def _delta_scan(local_out, bsz, seq):
    uf, ub, wf, wb, qf, qb, a, kt, ef, eb = local_out
    n, cols = uf.shape
    n_chunks = seq // CHUNK
    sub = min(SCAN_CHUNKS, n_chunks)
    steps = n_chunks // sub
    fwd = lambda b, c: (b * steps + c, 0)
    bwd = lambda b, c: (b * steps + (steps - 1 - c), 0)
    blk = lambda unit, idx: pl.BlockSpec((sub * unit, cols), idx)
    specs = lambda idx: [blk(CHUNK, idx), blk(CHUNK, idx), blk(CHUNK, idx), blk(CHUNK, idx),
                         blk(2 * CHUNK, idx), blk(SUBLANES, idx)]
    return pl.pallas_call(
        functools.partial(_delta_scan_body, sub_chunks=sub),
        grid=(bsz, steps),
        in_specs=specs(fwd) + specs(bwd),
        out_specs=[blk(CHUNK, fwd), blk(CHUNK, bwd)],
        out_shape=[jax.ShapeDtypeStruct((n, cols), F32), jax.ShapeDtypeStruct((n, cols), F32)],
        scratch_shapes=[pltpu.VMEM((2 * N_HEADS, HEAD_DIM, HEAD_DIM), F32)],
        compiler_params=_params("parallel", "arbitrary"),
        name="delta_scan",
    )(uf, wf, qf, a, kt, ef, ub, wb, qb, a, kt, eb)


def _mixer_out_body(of_ref, ob_ref, z_ref, yb_ref, ga_ref, gb_ref, x_ref,
                    onw_ref, wa_ref, wb_ref, wo_ref, gtm_ref, n2w_ref, scf_ref, shf_ref,
                    rwh_ref, rwl_ref, rb_ref, x1_ref, h2_ref, lg_ref, oz_scr):
    o = of_ref[...] + ob_ref[...]
    z = z_ref[...]
    onw = onw_ref[...]
    for h in range(N_HEADS):
        hs = slice(h * HEAD_DIM, (h + 1) * HEAD_DIM)
        oh = o[:, hs]
        ms = jnp.mean(oh * oh, axis=-1, keepdims=True)
        zh = z[:, hs]
        oz_scr[:, hs] = (oh * lax.rsqrt(ms + EPS) * onw * (zh * _sigmoid(zh))).astype(BF16)
    y_a = _dot(oz_scr[...], wa_ref[...])
    y_b = _dot(yb_ref[...].astype(BF16), wb_ref[...])
    merged = _sigmoid(ga_ref[...]) * y_a + _sigmoid(gb_ref[...]) * y_b
    x1 = x_ref[...] + gtm_ref[0] * _dot(merged.astype(BF16), wo_ref[...])
    x1_ref[...] = x1
    ms = jnp.mean(x1 * x1, axis=-1, keepdims=True)
    h2 = x1 * lax.rsqrt(ms + EPS) * n2w_ref[...] * (1.0 + scf_ref[0]) + shf_ref[0]
    _rows_to_tiles(h2_ref, h2)
    h_hi = h2.astype(BF16)
    h_lo = (h2 - h_hi.astype(F32)).astype(BF16)
    rwh = rwh_ref[...]
    lg_ref[...] = (_dot(h_hi, rwh) + (_dot(h_lo, rwh) + _dot(h_hi, rwl_ref[...]))) + rb_ref[...]


def _mixer_out(o_f, o_b, p_main, y_sc, x2d, onorm_w, w_up_a, w_out_sc, w_o, gt_m, norm2_w,
               sc_f, sh_f, router_w, router_b, seq, z_blk, ga_blk):
    n, d = x2d.shape
    tm = min(256, seq)
    tiles_per_batch = seq // tm
    row = lambda i: (i, 0)
    const = lambda i: (0, 0)
    per_b = lambda i: (i // tiles_per_batch, 0, 0)
    rw = jnp.pad(router_w, ((0, 0), (0, LANES - N_EXPERTS)))
    rw_hi = rw.astype(BF16)
    rw_lo = (rw - rw_hi.astype(F32)).astype(BF16)
    rb = jnp.pad(router_b.reshape(1, -1), ((0, 0), (0, LANES - N_EXPERTS)))
    return pl.pallas_call(
        _mixer_out_body,
        grid=(n // tm,),
        in_specs=[pl.BlockSpec((tm, d), row), pl.BlockSpec((tm, d), row),
                  pl.BlockSpec((tm, d), lambda i: (i, z_blk)), pl.BlockSpec((tm, d), row),
                  pl.BlockSpec((tm, d), lambda i: (i, ga_blk)),
                  pl.BlockSpec((tm, d), lambda i: (i, ga_blk + 1)), pl.BlockSpec((tm, d), row),
                  pl.BlockSpec((1, HEAD_DIM), const), pl.BlockSpec((d, d), const),
                  pl.BlockSpec((d, d), const), pl.BlockSpec((d, d), const),
                  pl.BlockSpec((1, 1, d), per_b), pl.BlockSpec((1, d), const),
                  pl.BlockSpec((1, 1, d), per_b), pl.BlockSpec((1, 1, d), per_b),
                  pl.BlockSpec((d, LANES), const), pl.BlockSpec((d, LANES), const),
                  pl.BlockSpec((1, LANES), const)],
        out_specs=[pl.BlockSpec((tm, d), row), pl.BlockSpec((tm * SUBLANES, LANES), row),
                   pl.BlockSpec((tm, LANES), row)],
        out_shape=[jax.ShapeDtypeStruct((n, d), F32),
                   jax.ShapeDtypeStruct((n * SUBLANES, LANES), F32),
                   jax.ShapeDtypeStruct((n, LANES), F32)],
        scratch_shapes=[pltpu.VMEM((tm, d), BF16)],
        compiler_params=_params("parallel"),
        name="mixer_out",
    )(o_f, o_b, p_main, y_sc, p_main, p_main, x2d, onorm_w.reshape(1, HEAD_DIM),
      w_up_a.astype(BF16), w_out_sc.astype(BF16), w_o.astype(BF16), gt_m,
      norm2_w.reshape(1, d), sc_f, sh_f, rw_hi, rw_lo, rb)


def _route_body(lg_ref, er_ref, gate_ref, cnt_ref, carry_scr):
    @pl.when(pl.program_id(0) == 0)
    def _():
        carry_scr[...] = jnp.zeros_like(carry_scr)

    rows = lg_ref.shape[0]
    lane = lax.broadcasted_iota(jnp.int32, (rows, LANES), 1)
    lanef = lane.astype(F32)
    logits = jnp.where(lane < N_EXPERTS, lg_ref[...], -jnp.inf)
    vals, idxs = [], []
    sel = jnp.zeros((rows, LANES), F32)
    for _ in range(TOP_K):
        m = jnp.max(logits, axis=-1, keepdims=True)
        idx = jnp.min(jnp.where(logits == m, lanef, float(LANES)), axis=-1, keepdims=True)
        hit = lanef == idx
        vals.append(m)
        idxs.append(idx)
        sel = sel + hit.astype(F32)
        logits = jnp.where(hit, -jnp.inf, logits)
    exps = [jnp.exp(v - vals[0]) for v in vals]
    total = exps[0]
    for e in exps[1:]:
        total = total + e
    r = lax.broadcasted_iota(jnp.int32, (rows, rows), 0)
    c = lax.broadcasted_iota(jnp.int32, (rows, rows), 1)
    earlier = (r > c).astype(BF16)
    carry = carry_scr[0:1, :]
    before = _dot(earlier, sel.astype(BF16)) + carry
    er = jnp.zeros((rows, LANES), F32)
    gates = jnp.zeros((rows, LANES), F32)
    for k in range(TOP_K):
        rank = jnp.sum(jnp.where(lanef == idxs[k], before, 0.0), axis=-1, keepdims=True)
        er = jnp.where(lane == k, idxs[k], er)
        er = jnp.where(lane == TOP_K + k, rank, er)
        gates = jnp.where(lane == k, exps[k] / total, gates)
    er_ref[...] = er
    gate_ref[...] = gates
    new_carry = carry + jnp.sum(sel, axis=0, keepdims=True)
    carry_scr[...] = jnp.broadcast_to(new_carry, carry_scr.shape)
    cnt_ref[...] = jnp.broadcast_to(new_carry, cnt_ref.shape)


def _route(logits):
    n = logits.shape[0]
    rows = min(256, n)
    tile = pl.BlockSpec((rows, LANES), lambda i: (i, 0))
    return pl.pallas_call(
        _route_body,
        grid=(n // rows,),
        in_specs=[tile],
        out_specs=[tile, tile, pl.BlockSpec((SUBLANES, LANES), lambda i: (0, 0))],
        out_shape=[jax.ShapeDtypeStruct((n, LANES), F32), jax.ShapeDtypeStruct((n, LANES), F32),
                   jax.ShapeDtypeStruct((SUBLANES, LANES), F32)],
        scratch_shapes=[pltpu.VMEM((SUBLANES, LANES), F32)],
        compiler_params=_params("arbitrary"),
        name="route",
    )(logits)


def _dest_body(er_ref, ps_ref, o_ref):
    rows = er_ref.shape[0]
    lane = lax.broadcasted_iota(jnp.int32, (rows, LANES), 1)
    lanef = lane.astype(F32)
    er = er_ref[...]
    starts = ps_ref[0:1, :]
    out = jnp.zeros((rows, LANES), F32)
    for k in range(TOP_K):
        e_k = jnp.sum(jnp.where(lane == k, er, 0.0), axis=-1, keepdims=True)
        r_k = jnp.sum(jnp.where(lane == TOP_K + k, er, 0.0), axis=-1, keepdims=True)
        base = jnp.sum(jnp.where(lanef == e_k, starts, 0.0), axis=-1, keepdims=True)
        out = jnp.where(lane == k, base + r_k, out)
    o_ref[...] = out.astype(jnp.int32)


def _dest_rows(er, pad_start_row):
    n = er.shape[0]
    rows = min(1024, n)
    tile = pl.BlockSpec((rows, LANES), lambda i: (i, 0))
    return pl.pallas_call(
        _dest_body,
        grid=(n // rows,),
        in_specs=[tile, pl.BlockSpec((SUBLANES, LANES), lambda i: (0, 0))],
        out_specs=tile,
        out_shape=jax.ShapeDtypeStruct((n, LANES), jnp.int32),
        compiler_params=_params("parallel"),
        name="dest_rows",
    )(er, pad_start_row)


def _tile_at(ref, row):
    return ref.at[pl.ds(pl.multiple_of(row * SUBLANES, SUBLANES), SUBLANES)]


def _dispatch_body(zero_ref, dest_ref, h_ref, xs_hbm, zbuf, sem, zsem):
    n_tok = dest_ref.shape[-1] // TOP_K
    blk_tiles = zbuf.shape[0]

    @pl.when(pl.program_id(0) == 0)
    def _():
        zbuf[...] = jnp.zeros_like(zbuf)

        def block_copy(b):
            return pltpu.make_async_copy(
                zbuf, xs_hbm.at[pl.ds(pl.multiple_of(b * blk_tiles, blk_tiles), blk_tiles)], zsem)

        def fill(b, carry):
            @pl.when(zero_ref[b] != 0)
            def _():
                block_copy(b).start()
            return carry

        def fill_done(b, carry):
            @pl.when(zero_ref[b] != 0)
            def _():
                block_copy(b).wait()
            return carry

        lax.fori_loop(0, zero_ref.shape[0], fill, 0)
        lax.fori_loop(0, zero_ref.shape[0], fill_done, 0)

    def issue(t, carry):
        for k in range(TOP_K):
            pltpu.make_async_copy(_tile_at(h_ref, t), _tile_at(xs_hbm, dest_ref[0, 0, t * TOP_K + k]),
                                  sem).start(priority=k % 2)
        return carry

    def drain(t, carry):
        for _ in range(TOP_K):
            pltpu.make_async_copy(_tile_at(h_ref, 0), _tile_at(xs_hbm, 0), sem).wait()
        return carry

    lax.fori_loop(0, n_tok, issue, 0)
    lax.fori_loop(0, n_tok, drain, 0)


def _dispatch(dest_flat, zero_blocks, h2_tiles, n_tok, m_pad):
    rows = min(256, n_tok)
    dest3 = dest_flat.reshape(n_tok // rows, 1, rows * TOP_K)
    grid_spec = pltpu.PrefetchScalarGridSpec(
        num_scalar_prefetch=1,
        grid=(n_tok // rows,),
        in_specs=[pl.BlockSpec((1, 1, rows * TOP_K), lambda i, zb: (i, 0, 0),
                               memory_space=pltpu.SMEM),
                  pl.BlockSpec((rows * SUBLANES, LANES), lambda i, zb: (i, 0))],
        out_specs=pl.BlockSpec(memory_space=pl.ANY),
        scratch_shapes=[pltpu.VMEM((MOE_ROWS * SUBLANES, LANES), h2_tiles.dtype),
                        pltpu.SemaphoreType.DMA, pltpu.SemaphoreType.DMA],
    )
    return pl.pallas_call(
        _dispatch_body,
        grid_spec=grid_spec,
        out_shape=jax.ShapeDtypeStruct((m_pad * SUBLANES, LANES), h2_tiles.dtype),
        compiler_params=_params("arbitrary"),
        name="dispatch",
    )(zero_blocks, dest3, h2_tiles)


def _expert_body(be_ref, nu_ref, first_ref, x_ref, w1_ref, b1_ref, w2_ref, b2_ref, o_ref,
                 w1_scr, w2_scr):
    del be_ref
    i = pl.program_id(0)

    @pl.when(first_ref[i] != 0)
    def _():
        def cast_rows(src, dst, r, n):
            rs = pl.ds(pl.multiple_of(r * n, n), n)
            dst[rs, :] = src[0, rs, :].astype(BF16)

        lax.fori_loop(0, w1_ref.shape[1] // LANES,
                      lambda r, c: (cast_rows(w1_ref, w1_scr, r, LANES), c)[1], 0)
        lax.fori_loop(0, w2_ref.shape[1] // LANES,
                      lambda r, c: (cast_rows(w2_ref, w2_scr, r, LANES), c)[1], 0)

    @pl.when(i < nu_ref[0])
    def _():
        d_ff = w2_ref.shape[1]
        x = _tiles_to_rows(x_ref, MOE_ROWS)
        gu = _dot(x.astype(BF16), w1_scr[...]) + b1_ref[0]
        gate = jnp.minimum(gu[:, :d_ff], SWIGLU_LIMIT)
        up = jnp.clip(gu[:, d_ff:], -SWIGLU_LIMIT, SWIGLU_LIMIT)
        act = gate * _sigmoid(SWIGLU_ALPHA * gate) * (up + 1.0)
        _rows_to_tiles(o_ref, _dot(act.astype(BF16), w2_scr[...]) + b2_ref[0])

    @pl.when(i >= nu_ref[0])
    def _():
        o_ref[...] = jnp.zeros_like(o_ref)


def _experts(xs_tiles, blk_exp, n_used, blk_first, w1, b1, w2, b2):
    n_exp, d, two_ff = w1.shape
    d_ff = two_ff // 2
    m_pad = xs_tiles.shape[0] // SUBLANES
    out_tiles = pl.BlockSpec((MOE_ROWS * SUBLANES, LANES), lambda i, be, nu, fi: (i, 0))
    in_tiles = pl.BlockSpec((MOE_ROWS * SUBLANES, LANES),
                            lambda i, be, nu, fi: (jnp.minimum(i, nu[0] - 1), 0))
    by_expert = lambda i, be, nu, fi: (be[i], 0, 0)
    grid_spec = pltpu.PrefetchScalarGridSpec(
        num_scalar_prefetch=3,
        grid=(m_pad // MOE_ROWS,),
        in_specs=[in_tiles,
                  pl.BlockSpec((1, d, two_ff), by_expert), pl.BlockSpec((1, 1, two_ff), by_expert),
                  pl.BlockSpec((1, d_ff, d), by_expert), pl.BlockSpec((1, 1, d), by_expert)],
        out_specs=out_tiles,
        scratch_shapes=[pltpu.VMEM((d, two_ff), BF16), pltpu.VMEM((d_ff, d), BF16)],
    )
    return pl.pallas_call(
        _expert_body,
        grid_spec=grid_spec,
        out_shape=jax.ShapeDtypeStruct((m_pad * SUBLANES, LANES), F32),
        compiler_params=_params("arbitrary", vmem=60 * 1024 * 1024),
        name="experts",
    )(blk_exp, n_used, blk_first, xs_tiles, w1, b1.reshape(n_exp, 1, two_ff), w2,
      b2.reshape(n_exp, 1, d))


def _combine_body(dcur_ref, dnext_ref, y_hbm, x1_ref, gate_ref, gtf_ref, fw_ref, o_ref, buf, sems):
    step = pl.program_id(0)
    half = x1_ref.shape[0] // 2

    def row_copy(d_ref, tile, slot, t, k):
        return pltpu.make_async_copy(
            _tile_at(y_hbm, d_ref[0, 0, (tile * half + t) * TOP_K + k]),
            _tile_at(buf.at[slot, k], t), sems.at[slot])

    def gather(d_ref, tile, slot):
        def issue(t, carry):
            for k in range(TOP_K):
                row_copy(d_ref, tile, slot, t, k).start(priority=k % 2)
            return carry
        lax.fori_loop(0, half, issue, 0)

    def finish(slot, rows):
        def drain(t, carry):
            for k in range(TOP_K):
                pltpu.make_async_copy(_tile_at(y_hbm, 0), _tile_at(buf.at[slot, k], 0),
                                      sems.at[slot]).wait()
            return carry
        lax.fori_loop(0, half, drain, 0)
        gates = gate_ref[rows, :]
        moe = gates[:, 0:1] * _tiles_to_rows(buf.at[slot, 0], half)
        for k in range(1, TOP_K):
            moe = moe + gates[:, k:k + 1] * _tiles_to_rows(buf.at[slot, k], half)
        x2 = x1_ref[rows, :] + gtf_ref[0] * moe
        ms = jnp.mean(x2 * x2, axis=-1, keepdims=True)
        o_ref[rows, :] = x2 * lax.rsqrt(ms + EPS) * fw_ref[...]

    @pl.when(step == 0)
    def _():
        gather(dcur_ref, 0, 0)

    gather(dcur_ref, 1, 1)
    finish(0, slice(0, half))

    @pl.when(step + 1 < pl.num_programs(0))
    def _():
        gather(dnext_ref, 0, 0)

    finish(1, slice(half, 2 * half))


def _combine(dest_flat, yb_tiles, x1, gates, gt_f, final_w, seq):
    n, d = x1.shape
    rows = min(256, seq)
    half = rows // 2
    steps = n // rows
    tiles_per_batch = seq // rows
    dest3 = dest_flat.reshape(steps, 1, rows * TOP_K)
    dest_blk = lambda idx: pl.BlockSpec((1, 1, rows * TOP_K), idx, memory_space=pltpu.SMEM)
    return pl.pallas_call(
        _combine_body,
        grid=(steps,),
        in_specs=[dest_blk(lambda i: (i, 0, 0)),
                  dest_blk(lambda i: (jnp.minimum(i + 1, steps - 1), 0, 0)),
                  pl.BlockSpec(memory_space=pl.ANY),
                  pl.BlockSpec((rows, d), lambda i: (i, 0)),
                  pl.BlockSpec((rows, LANES), lambda i: (i, 0)),
                  pl.BlockSpec((1, 1, d), lambda i: (i // tiles_per_batch, 0, 0)),
                  pl.BlockSpec((1, d), lambda i: (0, 0))],
        out_specs=pl.BlockSpec((rows, d), lambda i: (i, 0)),
        out_shape=jax.ShapeDtypeStruct((n, d), F32),
        scratch_shapes=[pltpu.VMEM((2, TOP_K, half * SUBLANES, LANES), F32),
                        pltpu.SemaphoreType.DMA((2,))],
        compiler_params=_params("arbitrary"),
        name="combine",
    )(dest3, dest3, yb_tiles, x1, gates, gt_f, final_w.reshape(1, d))


def _layer(x2d, c_mod, bsz, seq, norm1_w, w_in, conv_qkv_w, a_log, dt_bias, onorm_w, w_up_a,
           conv_sc_w, w_out_sc, w_o, norm2_w, router_w, router_b, moe_w1, moe_b1, moe_w2, moe_b2):
    n, d = x2d.shape
    dn = N_HEADS * HEAD_DIM
    sh_m, sc_m, gt_m, sh_f, sc_f, gt_f = (m.reshape(bsz, 1, d) for m in jnp.split(c_mod, 6, axis=-1))

    o_small, o_sc = 4 * dn, 4 * dn + 4 * N_HEADS
    w_main = jnp.concatenate([w_in[:, :o_small], w_in[:, o_sc:]], axis=1).astype(BF16)
    w_small = jnp.pad(w_in[:, o_small:o_sc], ((0, 0), (0, LANES - 4 * N_HEADS))).astype(BF16)
    p_main, p_small = _in_projection(x2d, norm1_w, sc_m, sh_m, w_main, w_small, seq)

    y_sc = _sc_conv(p_main, conv_sc_w, seq, first_col_block=4 * dn // d)
    local = _delta_local(p_main, p_small, conv_qkv_w, a_log, dt_bias, seq)
    o_f, o_b = _delta_scan(local, bsz, seq)
    x1, h2_tiles, logits = _mixer_out(o_f, o_b, p_main, y_sc, x2d, onorm_w, w_up_a, w_out_sc, w_o,
                                      gt_m, norm2_w, sc_f, sh_f, router_w, router_b, seq,
                                      z_blk=3 * dn // d, ga_blk=(4 * dn + 3 * d) // d)

    er, gates, counts = _route(logits)
    m_rows = n * TOP_K
    n_blocks = m_rows // MOE_ROWS + N_EXPERTS
    cnt = counts[0, :N_EXPERTS].astype(jnp.int32)
    padded = (cnt + MOE_ROWS - 1) // MOE_ROWS * MOE_ROWS
    pad_end = jnp.cumsum(padded)
    pad_start = pad_end - padded
    blk_row0 = jnp.arange(n_blocks, dtype=jnp.int32) * MOE_ROWS
    blk_exp = jnp.minimum(jnp.sum(pad_end[None, :] <= blk_row0[:, None], axis=1),
                          N_EXPERTS - 1).astype(jnp.int32)
    n_used = (pad_end[-1:] // MOE_ROWS).astype(jnp.int32)
    blk_first = jnp.concatenate([jnp.ones((1,), jnp.int32),
                                 (blk_exp[1:] != blk_exp[:-1]).astype(jnp.int32)])
    ends_padded = jnp.any((pad_end[None, :] == (blk_row0 + MOE_ROWS)[:, None])
                          & (padded != cnt)[None, :], axis=1)
    zero_blocks = (ends_padded | (blk_row0 >= pad_end[-1])).astype(jnp.int32)
    ps_row = jnp.broadcast_to(
        jnp.pad(pad_start.astype(F32), (0, LANES - N_EXPERTS)).reshape(1, LANES), (SUBLANES, LANES))
    dest = _dest_rows(er, ps_row)[:, :TOP_K].reshape(m_rows)

    xs_tiles = _dispatch(dest, zero_blocks, h2_tiles, n, n_blocks * MOE_ROWS)
    yb_tiles = _experts(xs_tiles, blk_exp, n_used, blk_first, moe_w1, moe_b1, moe_w2, moe_b2)
    return dest, yb_tiles, x1, gates, gt_f


def kernel(x, c, ada_w, ada_b, norm1_w, w_in, conv_qkv_w, a_log, dt_bias, onorm_w, w_up_a, conv_sc_w, w_out_sc, w_o, norm2_w, router_w, router_b, moe_w1, moe_b1, moe_w2, moe_b2, final_norm_w):
    bsz, seq, d = x.shape
    depth = ada_w.shape[0]
    assert depth == 1, "the combine stage fuses the final norm, which needs a single layer"
    x2d = x.reshape(bsz * seq, d)
    c_mod = _ada_mod(c, ada_w[0], ada_b[0])
    dest, yb_tiles, x1, gates, gt_f = _layer(
        x2d, c_mod, bsz, seq, norm1_w[0], w_in[0], conv_qkv_w[0], a_log[0], dt_bias[0], onorm_w[0],
        w_up_a[0], conv_sc_w[0], w_out_sc[0], w_o[0], norm2_w[0], router_w[0], router_b[0],
        moe_w1[0], moe_b1[0], moe_w2[0], moe_b2[0])
    out = _combine(dest, yb_tiles, x1, gates, gt_f, final_norm_w, seq)
    return out.reshape(bsz, seq, d)
```

```python
import functools

import jax
import jax.numpy as jnp
from jax import lax
from jax.experimental import pallas as pl
from jax.experimental.pallas import tpu as pltpu

F32 = jnp.float32
BF16 = jnp.bfloat16

EPS = 1e-6
N_HEADS = 8
HEAD_DIM = 128
CHUNK = 64
N_EXPERTS = 32
TOP_K = 4
SWIGLU_LIMIT = 7.0
SWIGLU_ALPHA = 1.702
LANES = 128
SUBLANES = 8
MOE_ROWS = 512
SCAN_CHUNKS = 4
VMEM_LIMIT = 56 * 1024 * 1024


def _sigmoid(x):
    return 1.0 / (1.0 + jnp.exp(-x))


def _softplus(x):
    return jnp.maximum(x, 0.0) + jnp.log(1.0 + jnp.exp(-jnp.abs(x)))


def _dot(a, b):
    return jnp.dot(a, b, preferred_element_type=F32)


def _bmm(a, b):
    return lax.dot_general(a, b, (((2,), (1,)), ((0,), (0,))), preferred_element_type=F32)


def _bmm_nt(a, b):
    return lax.dot_general(a, b, (((2,), (2,)), ((0,), (0,))), preferred_element_type=F32)


def _params(*sem, vmem=VMEM_LIMIT):
    return pltpu.CompilerParams(dimension_semantics=sem, vmem_limit_bytes=vmem)


def _rows_to_tiles(ref, value):
    n_rows = value.shape[0]
    for s in range(SUBLANES):
        ref[pl.ds(s, n_rows, stride=SUBLANES), :] = value[:, s * LANES:(s + 1) * LANES]


def _tiles_to_rows(ref, n_rows):
    return jnp.concatenate(
        [ref[pl.ds(s, n_rows, stride=SUBLANES), :] for s in range(SUBLANES)], axis=1)


def _ada_body(c_ref, w_ref, b_ref, o_ref):
    c = c_ref[...]
    ca = c * _sigmoid(c)
    o_ref[...] = _dot(ca.astype(BF16), w_ref[...].astype(BF16)) + b_ref[...]


def _ada_mod(c, ada_w, ada_b):
    bsz, d = c.shape
    n_out = ada_w.shape[1]
    cp = jnp.pad(c, ((0, SUBLANES - bsz), (0, 0)))
    out = pl.pallas_call(
        _ada_body,
        grid=(n_out // d,),
        in_specs=[pl.BlockSpec((SUBLANES, d), lambda j: (0, 0)),
                  pl.BlockSpec((d, d), lambda j: (0, j)),
                  pl.BlockSpec((1, d), lambda j: (0, j))],
        out_specs=pl.BlockSpec((SUBLANES, d), lambda j: (0, j)),
        out_shape=jax.ShapeDtypeStruct((SUBLANES, n_out), F32),
        compiler_params=_params("parallel"),
        name="ada_mod",
    )(cp, ada_w, ada_b.reshape(1, n_out))
    return out[:bsz]


def _inproj_body(x_ref, nw_ref, sc_ref, sh_ref, w_ref, ws_ref, o_ref, os_ref, h_scr):
    @pl.when(pl.program_id(1) == 0)
    def _():
        x = x_ref[...]
        ms = jnp.mean(x * x, axis=-1, keepdims=True)
        xn = x * lax.rsqrt(ms + EPS) * nw_ref[...]
        h = (xn * (1.0 + sc_ref[0]) + sh_ref[0]).astype(BF16)
        h_scr[...] = h
        os_ref[...] = _dot(h, ws_ref[...])

    o_ref[...] = _dot(h_scr[...], w_ref[...])


def _in_projection(x2d, norm_w, sc_m, sh_m, w_main, w_small, seq):
    n, d = x2d.shape
    width = w_main.shape[1]
    tm = min(2048, seq)
    tn = 1024
    tiles_per_batch = seq // tm
    return pl.pallas_call(
        _inproj_body,
        grid=(n // tm, width // tn),
        in_specs=[pl.BlockSpec((tm, d), lambda i, j: (i, 0)),
                  pl.BlockSpec((1, d), lambda i, j: (0, 0)),
                  pl.BlockSpec((1, 1, d), lambda i, j: (i // tiles_per_batch, 0, 0)),
                  pl.BlockSpec((1, 1, d), lambda i, j: (i // tiles_per_batch, 0, 0)),
                  pl.BlockSpec((d, tn), lambda i, j: (0, j)),
                  pl.BlockSpec((d, LANES), lambda i, j: (0, 0))],
        out_specs=[pl.BlockSpec((tm, tn), lambda i, j: (i, j)),
                   pl.BlockSpec((tm, LANES), lambda i, j: (i, 0))],
        out_shape=[jax.ShapeDtypeStruct((n, width), F32),
                   jax.ShapeDtypeStruct((n, LANES), F32)],
        scratch_shapes=[pltpu.VMEM((tm, d), BF16)],
        compiler_params=_params("parallel", "arbitrary"),
        name="in_proj",
    )(x2d, norm_w.reshape(1, d), sc_m, sh_m, w_main, w_small)


def _conv_staged(scr, prev, cur, nxt, cw, width):
    rows = cur.shape[0]
    scr[0:SUBLANES, :] = prev
    scr[SUBLANES:SUBLANES + rows, :] = cur
    scr[SUBLANES + rows:, :] = nxt
    pad = width // 2
    acc = scr[pl.ds(SUBLANES - pad, rows), :] * cw[0:1]
    for w in range(1, width):
        acc = acc + scr[pl.ds(SUBLANES - pad + w, rows), :] * cw[w:w + 1]
    return acc


def _halo_specs(rows, cols, col_block, n_rows):
    per = rows // SUBLANES
    last = n_rows // SUBLANES - 1
    cur = pl.BlockSpec((rows, cols), lambda i, j: (i, col_block(j)))
    prev = pl.BlockSpec((SUBLANES, cols), lambda i, j: (jnp.maximum(i * per - 1, 0), col_block(j)))
    nxt = pl.BlockSpec((SUBLANES, cols), lambda i, j: (jnp.minimum((i + 1) * per, last), col_block(j)))
    return cur, prev, nxt


def _sc_conv_body(b_ref, c_ref, cp_ref, cn_ref, u_ref, up_ref, un_ref, cw_ref, o_ref, *,
                  tiles_per_batch, width):
    t_in_b = pl.program_id(0) % tiles_per_batch
    rows = c_ref.shape[0]
    length = rows + 2 * SUBLANES
    prev = jnp.where(t_in_b == 0, 0.0, cp_ref[...] * up_ref[...])
    nxt = jnp.where(t_in_b == tiles_per_batch - 1, 0.0, cn_ref[...] * un_ref[...])
    ext = jnp.concatenate([prev, c_ref[...] * u_ref[...], nxt], axis=0)
    cw = cw_ref[...]
    pad = width // 2
    acc = ext * cw[pad:pad + 1]
    for w in range(width):
        if w != pad:
            acc = acc + pltpu.roll(ext, (pad - w) % length, axis=0) * cw[w:w + 1]
    o_ref[...] = b_ref[...] * acc[SUBLANES:SUBLANES + rows]


def _sc_conv(p_main, conv_w, seq, first_col_block):
    n = p_main.shape[0]
    width, ch = conv_w.shape
    rows = min(512, seq)
    cw = jnp.pad(conv_w, ((0, SUBLANES - width), (0, 0)))
    b_spec = pl.BlockSpec((rows, ch), lambda i, j: (i, first_col_block))
    c_specs = _halo_specs(rows, ch, lambda j: first_col_block + 1, n)
    u_specs = _halo_specs(rows, ch, lambda j: first_col_block + 2, n)
    body = functools.partial(_sc_conv_body, tiles_per_batch=seq // rows, width=width)
    return pl.pallas_call(
        body,
        grid=(n // rows, 1),
        in_specs=[b_spec, *c_specs, *u_specs, pl.BlockSpec((SUBLANES, ch), lambda i, j: (0, 0))],
        out_specs=pl.BlockSpec((rows, ch), lambda i, j: (i, 0)),
        out_shape=jax.ShapeDtypeStruct((n, ch), F32),
        compiler_params=_params("parallel", "arbitrary"),
        name="sc_conv",
    )(p_main, p_main, p_main, p_main, p_main, p_main, p_main, cw)


def _block_diag(x, lo_half):
    return jnp.concatenate([jnp.where(lo_half, x, 0.0), jnp.where(lo_half, 0.0, x)], axis=1)


def _delta_local_body(qc_ref, qp_ref, qn_ref, qw_ref, kc_ref, kp_ref, kn_ref, kw_ref,
                      vc_ref, vp_ref, vn_ref, vw_ref, s_ref, alog_ref, dtb_ref,
                      uf_ref, ub_ref, wf_ref, wb_ref, qf_ref, qb_ref, a_ref, kt_ref, ef_ref, eb_ref,
                      beta_scr, g_scr, gl_scr, gt_scr, gtr_scr, cq_scr, ck_scr, cv_scr, *,
                      tiles_per_batch, width):
    h = pl.program_id(1)
    rows = qc_ref.shape[0]
    n_chunks = rows // CHUNK
    lane = lax.broadcasted_iota(jnp.int32, (rows, LANES), 1)

    @pl.when(h == 0)
    def _():
        s = s_ref[...]
        beta_scr[...] = _sigmoid(s)
        g = -jnp.exp(alog_ref[...]) * _softplus(s + dtb_ref[...])
        pos = lax.broadcasted_iota(jnp.int32, (rows, LANES), 0) & (CHUNK - 1)
        gp = g
        gs = g
        step = 1
        while step < CHUNK:
            gp = gp + jnp.where(pos >= step, pltpu.roll(gp, step, axis=0), 0.0)
            gs = gs + jnp.where(pos < CHUNK - step, pltpu.roll(gs, rows - step, axis=0), 0.0)
            step *= 2
        gc = jnp.where(lane >= 3 * N_HEADS, gs, gp)
        g_scr[...] = gc
        gl_scr[...] = gp + gs - g
        for c2 in range(rows // LANES):
            gt = gc[c2 * LANES:(c2 + 1) * LANES, :].T
            gt_scr[c2] = gt
            gtr_scr[c2] = pltpu.roll(gt, CHUNK, axis=1)

    t_in_b = pl.program_id(0) % tiles_per_batch

    def conv_in(cur_ref, prev_ref, next_ref, w_ref, scr, normed):
        prev = jnp.where(t_in_b == 0, 0.0, prev_ref[...])
        nxt = jnp.where(t_in_b == tiles_per_batch - 1, 0.0, next_ref[...])
        y = _conv_staged(scr, prev, cur_ref[...], nxt, w_ref[...], width)
        y = y * _sigmoid(y)
        if normed:
            y = y * lax.rsqrt(jnp.sum(y * y, axis=-1, keepdims=True) + EPS)
        return y

    q2 = conv_in(qc_ref, qp_ref, qn_ref, qw_ref, cq_scr, True) * (HEAD_DIM ** -0.5)
    k2 = conv_in(kc_ref, kp_ref, kn_ref, kw_ref, ck_scr, True)
    v2 = conv_in(vc_ref, vp_ref, vn_ref, vw_ref, cv_scr, False)

    def pick(ref, idx):
        return jnp.sum(jnp.where(lane == idx, ref[...], 0.0), axis=-1, keepdims=True)

    i_f, i_b = 2 * N_HEADS + h, 3 * N_HEADS + h
    beta_f, beta_b = pick(beta_scr, h), pick(beta_scr, N_HEADS + h)
    g_f, g_b = pick(g_scr, i_f), pick(g_scr, i_b)
    gl_f, gl_b = pick(gl_scr, i_f), pick(gl_scr, i_b)

    kb_f, kb_b = k2 * beta_f, k2 * beta_b
    eg_f, eg_b = jnp.exp(g_f), jnp.exp(g_b)
    to3 = lambda a: a.reshape(n_chunks, CHUNK, a.shape[-1])
    bf3 = lambda a: to3(a.astype(BF16))

    k3 = bf3(k2)
    zero3 = jnp.zeros_like(k3)
    lhs = jnp.concatenate([jnp.concatenate([bf3(kb_f), bf3(kb_b)], axis=2),
                           jnp.concatenate([bf3(q2), bf3(q2)], axis=2)], axis=1)
    rhs_t = jnp.concatenate([jnp.concatenate([k3, zero3], axis=2),
                             jnp.concatenate([zero3, k3], axis=2)], axis=1)
    m1 = _bmm_nt(lhs, rhs_t)

    ii = lax.broadcasted_iota(jnp.int32, (CHUNK, LANES), 0)
    l2 = lax.broadcasted_iota(jnp.int32, (CHUNK, LANES), 1)
    jj = l2 & (CHUNK - 1)
    lo_half = l2 < CHUNK
    ahead = jnp.where(lo_half, ii - jj, jj - ii)
    incl = ahead >= 0
    strict = ahead > 0
    eye = (ii == jj).astype(F32)

    g_cols = jnp.where(lane < CHUNK, g_f, g_b)
    lrow = lax.broadcasted_iota(jnp.int32, (1, LANES), 1) < CHUNK
    diffs = []
    for c in range(n_chunks):
        rf = gt_scr[c // 2, pl.ds(i_f, 1), :]
        rb = gt_scr[c // 2, pl.ds(i_b, 1), :]
        if c % 2 == 0:
            g_row = jnp.where(lrow, rf, gtr_scr[c // 2, pl.ds(i_b, 1), :])
        else:
            g_row = jnp.where(lrow, gtr_scr[c // 2, pl.ds(i_f, 1), :], rb)
        diffs.append(g_cols[c * CHUNK:(c + 1) * CHUNK] - g_row)
    decay = jnp.exp(jnp.where(incl, jnp.stack(diffs), -jnp.inf))

    nm = -jnp.where(strict, m1[:, :CHUNK] * decay, 0.0)
    a_ref[...] = (m1[:, CHUNK:] * decay).astype(BF16).reshape(rows, LANES)

    p = eye + nm
    npow = _bmm(nm.astype(BF16), _block_diag(nm, lo_half).astype(BF16))
    n_iter = CHUNK.bit_length() - 2
    for it in range(n_iter):
        bd = _block_diag(npow, lo_half).astype(BF16)
        if it + 1 < n_iter:
            x = _bmm(jnp.concatenate([p, npow], axis=1).astype(BF16), bd)
            p = p + x[:, :CHUNK]
            npow = x[:, CHUNK:]
        else:
            p = p + _bmm(p.astype(BF16), bd)

    rhs = jnp.concatenate([jnp.concatenate([bf3(v2 * beta_f), bf3(kb_f * eg_f)], axis=2),
                           jnp.concatenate([bf3(v2 * beta_b), bf3(kb_b * eg_b)], axis=2)], axis=1)
    uw = _bmm(_block_diag(p, lo_half).astype(BF16), rhs)
    uf_ref[...] = uw[:, :CHUNK, :HEAD_DIM].reshape(rows, HEAD_DIM)
    ub_ref[...] = uw[:, CHUNK:, :HEAD_DIM].reshape(rows, HEAD_DIM)
    wf_ref[...] = uw[:, :CHUNK, HEAD_DIM:].astype(BF16).reshape(rows, HEAD_DIM)
    wb_ref[...] = uw[:, CHUNK:, HEAD_DIM:].astype(BF16).reshape(rows, HEAD_DIM)
    qf_ref[...] = (q2 * eg_f).astype(BF16)
    qb_ref[...] = (q2 * eg_b).astype(BF16)

    kd_f = to3(k2 * jnp.exp(gl_f - g_f))
    kd_b = to3(k2 * jnp.exp(gl_b - g_b))
    e_f = to3(jnp.broadcast_to(jnp.exp(gl_f), (rows, LANES)))
    e_b = to3(jnp.broadcast_to(jnp.exp(gl_b), (rows, LANES)))
    ef_ref[...] = e_f[:, :SUBLANES, :].reshape(n_chunks * SUBLANES, LANES)
    eb_ref[...] = e_b[:, :SUBLANES, :].reshape(n_chunks * SUBLANES, LANES)
    for c in range(n_chunks):
        kt_ref[c * 2 * CHUNK:(c + 1) * 2 * CHUNK, :] = (
            jnp.concatenate([kd_f[c], kd_b[c]], axis=0).T.astype(BF16))


def _delta_local(p_main, p_small, conv_w, a_log, dt_bias, seq):
    n = p_main.shape[0]
    width = conv_w.shape[0]
    cols = N_HEADS * HEAD_DIM
    rows = min(1024, seq)
    cw = jnp.pad(conv_w, ((0, SUBLANES - width), (0, 0)))
    alog_row = jnp.pad(a_log.reshape(1, -1), ((0, 0), (2 * N_HEADS, LANES - 4 * N_HEADS)))
    dtb_row = jnp.pad(dt_bias.reshape(1, -1), ((0, 0), (2 * N_HEADS, LANES - 4 * N_HEADS)))

    def qkv_specs(off):
        col = lambda h: off + h
        return [*_halo_specs(rows, HEAD_DIM, col, n),
                pl.BlockSpec((SUBLANES, HEAD_DIM), lambda i, h: (0, off + h))]

    row_spec = pl.BlockSpec((1, LANES), lambda i, h: (0, 0))
    out_blk = pl.BlockSpec((rows, HEAD_DIM), lambda i, h: (i, h))
    f32_out = jax.ShapeDtypeStruct((n, cols), F32)
    bf_out = jax.ShapeDtypeStruct((n, cols), BF16)
    eg_out = jax.ShapeDtypeStruct((n // SUBLANES, cols), F32)
    eg_blk = pl.BlockSpec((rows // SUBLANES, HEAD_DIM), lambda i, h: (i, h))
    tile = pltpu.VMEM((rows, LANES), F32)
    tposed = pltpu.VMEM((rows // LANES, LANES, LANES), F32)
    staged = pltpu.VMEM((rows + 2 * SUBLANES, HEAD_DIM), F32)
    body = functools.partial(_delta_local_body, tiles_per_batch=seq // rows, width=width)
    qkv_in = [p_main, p_main, p_main, cw]
    return pl.pallas_call(
        body,
        grid=(n // rows, N_HEADS),
        in_specs=[*qkv_specs(0), *qkv_specs(N_HEADS), *qkv_specs(2 * N_HEADS),
                  pl.BlockSpec((rows, LANES), lambda i, h: (i, 0)), row_spec, row_spec],
        out_specs=[out_blk, out_blk, out_blk, out_blk, out_blk, out_blk, out_blk,
                   pl.BlockSpec((2 * rows, HEAD_DIM), lambda i, h: (i, h)), eg_blk, eg_blk],
        out_shape=[f32_out, f32_out, bf_out, bf_out, bf_out, bf_out, bf_out,
                   jax.ShapeDtypeStruct((2 * n, cols), BF16), eg_out, eg_out],
        scratch_shapes=[tile, tile, tile, tposed, tposed, staged, staged, staged],
        compiler_params=_params("parallel", "arbitrary"),
        name="delta_local",
    )(*qkv_in, *qkv_in, *qkv_in, p_small, alog_row, dtb_row)


def _delta_scan_body(uf_ref, wf_ref, qf_ref, af_ref, ktf_ref, ef_ref,
                     ub_ref, wb_ref, qb_ref, ab_ref, ktb_ref, eb_ref,
                     of_ref, ob_ref, s_scr, *, sub_chunks):
    @pl.when(pl.program_id(1) == 0)
    def _():
        s_scr[...] = jnp.zeros_like(s_scr)

    heads = [slice(h * HEAD_DIM, (h + 1) * HEAD_DIM) for h in range(N_HEADS)]
    state = s_scr[...]
    for j in range(sub_chunks):
        jb = sub_chunks - 1 - j

        def per_head(ref_f, ref_b, unit, rows=None, half=False):
            rows = unit if rows is None else rows
            lo, hi = ((slice(0, CHUNK), slice(CHUNK, 2 * CHUNK)) if half
                      else (slice(None), slice(None)))
            return jnp.stack([ref_f[j * unit:j * unit + rows, hs][:, lo] for hs in heads]
                             + [ref_b[jb * unit:jb * unit + rows, hs][:, hi] for hs in heads])

        wq = jnp.concatenate([per_head(wf_ref, wb_ref, CHUNK), per_head(qf_ref, qb_ref, CHUNK)],
                             axis=1)
        x = _bmm(wq, state.astype(BF16))
        v_new = (per_head(uf_ref, ub_ref, CHUNK) - x[:, :CHUNK]).astype(BF16)
        lhs = jnp.concatenate([per_head(af_ref, ab_ref, CHUNK, half=True),
                               per_head(ktf_ref, ktb_ref, 2 * CHUNK, half=True)], axis=1)
        y = _bmm(lhs, v_new)
        out = x[:, CHUNK:] + y[:, :CHUNK]
        for h, hs in enumerate(heads):
            of_ref[j * CHUNK:(j + 1) * CHUNK, hs] = out[h]
            ob_ref[jb * CHUNK:(jb + 1) * CHUNK, hs] = out[N_HEADS + h]
        state = state * per_head(ef_ref, eb_ref, SUBLANES, rows=1) + y[:, CHUNK:]
    s_scr[...] = state


def _delta_scan(local_out, bsz, seq):
    uf, ub, wf, wb, qf, qb, a, kt, ef, eb = local_out
    n, cols = uf.shape
    n_chunks = seq // CHUNK
    sub = min(SCAN_CHUNKS, n_chunks)
    steps = n_chunks // sub
    fwd = lambda b, c: (b * steps + c, 0)
    bwd = lambda b, c: (b * steps + (steps - 1 - c), 0)
    blk = lambda unit, idx: pl.BlockSpec((sub * unit, cols), idx)
    specs = lambda idx: [blk(CHUNK, idx), blk(CHUNK, idx), blk(CHUNK, idx), blk(CHUNK, idx),
                         blk(2 * CHUNK, idx), blk(SUBLANES, idx)]
    return pl.pallas_call(
        functools.partial(_delta_scan_body, sub_chunks=sub),
        grid=(bsz, steps),
        in_specs=specs(fwd) + specs(bwd),
        out_specs=[blk(CHUNK, fwd), blk(CHUNK, bwd)],
        out_shape=[jax.ShapeDtypeStruct((n, cols), F32), jax.ShapeDtypeStruct((n, cols), F32)],
        scratch_shapes=[pltpu.VMEM((2 * N_HEADS, HEAD_DIM, HEAD_DIM), F32)],
        compiler_params=_params("parallel", "arbitrary"),
        name="delta_scan",
    )(uf, wf, qf, a, kt, ef, ub, wb, qb, a, kt, eb)


def _mixer_out_body(of_ref, ob_ref, z_ref, yb_ref, ga_ref, gb_ref, x_ref,
                    onw_ref, wa_ref, wb_ref, wo_ref, gtm_ref, n2w_ref, scf_ref, shf_ref,
                    rwh_ref, rwl_ref, rb_ref, x1_ref, h2_ref, lg_ref, oz_scr):
    o = of_ref[...] + ob_ref[...]
    z = z_ref[...]
    onw = onw_ref[...]
    for h in range(N_HEADS):
        hs = slice(h * HEAD_DIM, (h + 1) * HEAD_DIM)
        oh = o[:, hs]
        ms = jnp.mean(oh * oh, axis=-1, keepdims=True)
        zh = z[:, hs]
        oz_scr[:, hs] = (oh * lax.rsqrt(ms + EPS) * onw * (zh * _sigmoid(zh))).astype(BF16)
    y_a = _dot(oz_scr[...], wa_ref[...])
    y_b = _dot(yb_ref[...].astype(BF16), wb_ref[...])
    merged = _sigmoid(ga_ref[...]) * y_a + _sigmoid(gb_ref[...]) * y_b
    x1 = x_ref[...] + gtm_ref[0] * _dot(merged.astype(BF16), wo_ref[...])
    x1_ref[...] = x1
    ms = jnp.mean(x1 * x1, axis=-1, keepdims=True)
    h2 = x1 * lax.rsqrt(ms + EPS) * n2w_ref[...] * (1.0 + scf_ref[0]) + shf_ref[0]
    _rows_to_tiles(h2_ref, h2)
    h_hi = h2.astype(BF16)
    h_lo = (h2 - h_hi.astype(F32)).astype(BF16)
    rwh = rwh_ref[...]
    lg_ref[...] = (_dot(h_hi, rwh) + (_dot(h_lo, rwh) + _dot(h_hi, rwl_ref[...]))) + rb_ref[...]


def _mixer_out(o_f, o_b, p_main, y_sc, x2d, onorm_w, w_up_a, w_out_sc, w_o, gt_m, norm2_w,
               sc_f, sh_f, router_w, router_b, seq, z_blk, ga_blk):
    n, d = x2d.shape
    tm = min(512, seq)
    tiles_per_batch = seq // tm
    row = lambda i: (i, 0)
    const = lambda i: (0, 0)
    per_b = lambda i: (i // tiles_per_batch, 0, 0)
    weight = pl.BlockSpec((d, d), const, pipeline_mode=pl.Buffered(1))
    rw = jnp.pad(router_w, ((0, 0), (0, LANES - N_EXPERTS)))
    rw_hi = rw.astype(BF16)
    rw_lo = (rw - rw_hi.astype(F32)).astype(BF16)
    rb = jnp.pad(router_b.reshape(1, -1), ((0, 0), (0, LANES - N_EXPERTS)))
    return pl.pallas_call(
        _mixer_out_body,
        grid=(n // tm,),
        in_specs=[pl.BlockSpec((tm, d), row), pl.BlockSpec((tm, d), row),
                  pl.BlockSpec((tm, d), lambda i: (i, z_blk)), pl.BlockSpec((tm, d), row),
                  pl.BlockSpec((tm, d), lambda i: (i, ga_blk)),
                  pl.BlockSpec((tm, d), lambda i: (i, ga_blk + 1)), pl.BlockSpec((tm, d), row),
                  pl.BlockSpec((1, HEAD_DIM), const), weight, weight, weight,
                  pl.BlockSpec((1, 1, d), per_b), pl.BlockSpec((1, d), const),
                  pl.BlockSpec((1, 1, d), per_b), pl.BlockSpec((1, 1, d), per_b),
                  pl.BlockSpec((d, LANES), const), pl.BlockSpec((d, LANES), const),
                  pl.BlockSpec((1, LANES), const)],
        out_specs=[pl.BlockSpec((tm, d), row), pl.BlockSpec((tm * SUBLANES, LANES), row),
                   pl.BlockSpec((tm, LANES), row)],
        out_shape=[jax.ShapeDtypeStruct((n, d), F32),
                   jax.ShapeDtypeStruct((n * SUBLANES, LANES), F32),
                   jax.ShapeDtypeStruct((n, LANES), F32)],
        scratch_shapes=[pltpu.VMEM((tm, d), BF16)],
        compiler_params=_params("parallel"),
        name="mixer_out",
    )(o_f, o_b, p_main, y_sc, p_main, p_main, x2d, onorm_w.reshape(1, HEAD_DIM),
      w_up_a.astype(BF16), w_out_sc.astype(BF16), w_o.astype(BF16), gt_m,
      norm2_w.reshape(1, d), sc_f, sh_f, rw_hi, rw_lo, rb)


def _route_body(lg_ref, er_ref, gate_ref, cnt_ref, carry_scr):
    @pl.when(pl.program_id(0) == 0)
    def _():
        carry_scr[...] = jnp.zeros_like(carry_scr)

    rows = lg_ref.shape[0]
    lane = lax.broadcasted_iota(jnp.int32, (rows, LANES), 1)
    lanef = lane.astype(F32)
    logits = jnp.where(lane < N_EXPERTS, lg_ref[...], -jnp.inf)
    vals, idxs = [], []
    sel = jnp.zeros((rows, LANES), F32)
    for _ in range(TOP_K):
        m = jnp.max(logits, axis=-1, keepdims=True)
        idx = jnp.min(jnp.where(logits == m, lanef, float(LANES)), axis=-1, keepdims=True)
        hit = lanef == idx
        vals.append(m)
        idxs.append(idx)
        sel = sel + hit.astype(F32)
        logits = jnp.where(hit, -jnp.inf, logits)
    exps = [jnp.exp(v - vals[0]) for v in vals]
    total = exps[0]
    for e in exps[1:]:
        total = total + e
    r = lax.broadcasted_iota(jnp.int32, (rows, rows), 0)
    c = lax.broadcasted_iota(jnp.int32, (rows, rows), 1)
    earlier = (r > c).astype(BF16)
    carry = carry_scr[0:1, :]
    before = _dot(earlier, sel.astype(BF16)) + carry
    er = jnp.zeros((rows, LANES), F32)
    gates = jnp.zeros((rows, LANES), F32)
    for k in range(TOP_K):
        rank = jnp.sum(jnp.where(lanef == idxs[k], before, 0.0), axis=-1, keepdims=True)
        er = jnp.where(lane == k, idxs[k], er)
        er = jnp.where(lane == TOP_K + k, rank, er)
        gates = jnp.where(lane == k, exps[k] / total, gates)
    er_ref[...] = er
    gate_ref[...] = gates
    new_carry = carry + jnp.sum(sel, axis=0, keepdims=True)
    carry_scr[...] = jnp.broadcast_to(new_carry, carry_scr.shape)
    cnt_ref[...] = jnp.broadcast_to(new_carry, cnt_ref.shape)


def _route(logits):
    n = logits.shape[0]
    rows = min(256, n)
    tile = pl.BlockSpec((rows, LANES), lambda i: (i, 0))
    return pl.pallas_call(
        _route_body,
        grid=(n // rows,),
        in_specs=[tile],
        out_specs=[tile, tile, pl.BlockSpec((SUBLANES, LANES), lambda i: (0, 0))],
        out_shape=[jax.ShapeDtypeStruct((n, LANES), F32), jax.ShapeDtypeStruct((n, LANES), F32),
                   jax.ShapeDtypeStruct((SUBLANES, LANES), F32)],
        scratch_shapes=[pltpu.VMEM((SUBLANES, LANES), F32)],
        compiler_params=_params("arbitrary"),
        name="route",
    )(logits)


def _dest_body(er_ref, ps_ref, o_ref):
    rows = er_ref.shape[0]
    lane = lax.broadcasted_iota(jnp.int32, (rows, LANES), 1)
    lanef = lane.astype(F32)
    er = er_ref[...]
    starts = ps_ref[0:1, :]
    out = jnp.zeros((rows, LANES), F32)
    for k in range(TOP_K):
        e_k = jnp.sum(jnp.where(lane == k, er, 0.0), axis=-1, keepdims=True)
        r_k = jnp.sum(jnp.where(lane == TOP_K + k, er, 0.0), axis=-1, keepdims=True)
        base = jnp.sum(jnp.where(lanef == e_k, starts, 0.0), axis=-1, keepdims=True)
        out = jnp.where(lane == k, base + r_k, out)
    o_ref[...] = out.astype(jnp.int32)


def _dest_rows(er, pad_start_row):
    n = er.shape[0]
    rows = min(1024, n)
    tile = pl.BlockSpec((rows, LANES), lambda i: (i, 0))
    return pl.pallas_call(
        _dest_body,
        grid=(n // rows,),
        in_specs=[tile, pl.BlockSpec((SUBLANES, LANES), lambda i: (0, 0))],
        out_specs=tile,
        out_shape=jax.ShapeDtypeStruct((n, LANES), jnp.int32),
        compiler_params=_params("parallel"),
        name="dest_rows",
    )(er, pad_start_row)


def _tile_at(ref, row):
    return ref.at[pl.ds(pl.multiple_of(row * SUBLANES, SUBLANES), SUBLANES)]


def _dispatch_body(zero_ref, dest_ref, h_ref, xs_hbm, zbuf, sem, zsem):
    n_tok = dest_ref.shape[-1] // TOP_K
    blk_tiles = zbuf.shape[0]

    @pl.when(pl.program_id(0) == 0)
    def _():
        zbuf[...] = jnp.zeros_like(zbuf)

        def block_copy(b):
            return pltpu.make_async_copy(
                zbuf, xs_hbm.at[pl.ds(pl.multiple_of(b * blk_tiles, blk_tiles), blk_tiles)], zsem)

        def fill(b, carry):
            @pl.when(zero_ref[b] != 0)
            def _():
                block_copy(b).start()
            return carry

        def fill_done(b, carry):
            @pl.when(zero_ref[b] != 0)
            def _():
                block_copy(b).wait()
            return carry

        lax.fori_loop(0, zero_ref.shape[0], fill, 0)
        lax.fori_loop(0, zero_ref.shape[0], fill_done, 0)

    def issue(t, carry):
        for k in range(TOP_K):
            pltpu.make_async_copy(_tile_at(h_ref, t), _tile_at(xs_hbm, dest_ref[0, 0, t * TOP_K + k]),
                                  sem).start(priority=k % 2)
        return carry

    lax.fori_loop(0, n_tok, issue, 0)
    for _ in range(TOP_K):
        pltpu.make_async_copy(h_ref, xs_hbm.at[pl.ds(0, h_ref.shape[0])], sem).wait()


def _dispatch(dest_flat, zero_blocks, h2_tiles, n_tok, m_pad):
    rows = min(256, n_tok)
    dest3 = dest_flat.reshape(n_tok // rows, 1, rows * TOP_K)
    grid_spec = pltpu.PrefetchScalarGridSpec(
        num_scalar_prefetch=1,
        grid=(n_tok // rows,),
        in_specs=[pl.BlockSpec((1, 1, rows * TOP_K), lambda i, zb: (i, 0, 0),
                               memory_space=pltpu.SMEM),
                  pl.BlockSpec((rows * SUBLANES, LANES), lambda i, zb: (i, 0))],
        out_specs=pl.BlockSpec(memory_space=pl.ANY),
        scratch_shapes=[pltpu.VMEM((MOE_ROWS * SUBLANES, LANES), h2_tiles.dtype),
                        pltpu.SemaphoreType.DMA, pltpu.SemaphoreType.DMA],
    )
    return pl.pallas_call(
        _dispatch_body,
        grid_spec=grid_spec,
        out_shape=jax.ShapeDtypeStruct((m_pad * SUBLANES, LANES), h2_tiles.dtype),
        compiler_params=_params("arbitrary"),
        name="dispatch",
    )(zero_blocks, dest3, h2_tiles)


def _expert_body(be_ref, nu_ref, first_ref, x_ref, w1_ref, b1_ref, w2_ref, b2_ref, o_ref,
                 w1_scr, w2_scr):
    del be_ref
    i = pl.program_id(0)

    @pl.when(first_ref[i] != 0)
    def _():
        def cast_rows(src, dst, r, n):
            rs = pl.ds(pl.multiple_of(r * n, n), n)
            dst[rs, :] = src[0, rs, :].astype(BF16)

        lax.fori_loop(0, w1_ref.shape[1] // LANES,
                      lambda r, c: (cast_rows(w1_ref, w1_scr, r, LANES), c)[1], 0)
        lax.fori_loop(0, w2_ref.shape[1] // LANES,
                      lambda r, c: (cast_rows(w2_ref, w2_scr, r, LANES), c)[1], 0)

    @pl.when(i < nu_ref[0])
    def _():
        d_ff = w2_ref.shape[1]
        x = _tiles_to_rows(x_ref, MOE_ROWS)
        gu = _dot(x.astype(BF16), w1_scr[...]) + b1_ref[0]
        gate = jnp.minimum(gu[:, :d_ff], SWIGLU_LIMIT)
        up = jnp.clip(gu[:, d_ff:], -SWIGLU_LIMIT, SWIGLU_LIMIT)
        act = gate * _sigmoid(SWIGLU_ALPHA * gate) * (up + 1.0)
        _rows_to_tiles(o_ref, _dot(act.astype(BF16), w2_scr[...]) + b2_ref[0])

    @pl.when(i >= nu_ref[0])
    def _():
        o_ref[...] = jnp.zeros_like(o_ref)


def _experts(xs_tiles, blk_exp, n_used, blk_first, w1, b1, w2, b2):
    n_exp, d, two_ff = w1.shape
    d_ff = two_ff // 2
    m_pad = xs_tiles.shape[0] // SUBLANES
    out_tiles = pl.BlockSpec((MOE_ROWS * SUBLANES, LANES), lambda i, be, nu, fi: (i, 0))
    in_tiles = pl.BlockSpec((MOE_ROWS * SUBLANES, LANES),
                            lambda i, be, nu, fi: (jnp.minimum(i, nu[0] - 1), 0))
    by_expert = lambda i, be, nu, fi: (be[i], 0, 0)
    grid_spec = pltpu.PrefetchScalarGridSpec(
        num_scalar_prefetch=3,
        grid=(m_pad // MOE_ROWS,),
        in_specs=[in_tiles,
                  pl.BlockSpec((1, d, two_ff), by_expert), pl.BlockSpec((1, 1, two_ff), by_expert),
                  pl.BlockSpec((1, d_ff, d), by_expert), pl.BlockSpec((1, 1, d), by_expert)],
        out_specs=out_tiles,
        scratch_shapes=[pltpu.VMEM((d, two_ff), BF16), pltpu.VMEM((d_ff, d), BF16)],
    )
    return pl.pallas_call(
        _expert_body,
        grid_spec=grid_spec,
        out_shape=jax.ShapeDtypeStruct((m_pad * SUBLANES, LANES), F32),
        compiler_params=_params("arbitrary", vmem=60 * 1024 * 1024),
        name="experts",
    )(blk_exp, n_used, blk_first, xs_tiles, w1, b1.reshape(n_exp, 1, two_ff), w2,
      b2.reshape(n_exp, 1, d))


def _combine_body(dcur_ref, dnext_ref, y_hbm, x1_ref, gate_ref, gtf_ref, fw_ref, o_ref, buf, sems):
    step = pl.program_id(0)
    half = x1_ref.shape[0] // 2

    def row_copy(d_ref, tile, slot, t, k):
        return pltpu.make_async_copy(
            _tile_at(y_hbm, d_ref[0, 0, (tile * half + t) * TOP_K + k]),
            _tile_at(buf.at[slot, k], t), sems.at[slot])

    def gather(d_ref, tile, slot):
        def issue(t, carry):
            for k in range(TOP_K):
                row_copy(d_ref, tile, slot, t, k).start(priority=k % 2)
            return carry
        lax.fori_loop(0, half, issue, 0)

    def finish(slot, rows):
        for k in range(TOP_K):
            pltpu.make_async_copy(y_hbm.at[pl.ds(0, half * SUBLANES)], buf.at[slot, k],
                                  sems.at[slot]).wait()
        gates = gate_ref[rows, :]
        moe = gates[:, 0:1] * _tiles_to_rows(buf.at[slot, 0], half)
        for k in range(1, TOP_K):
            moe = moe + gates[:, k:k + 1] * _tiles_to_rows(buf.at[slot, k], half)
        x2 = x1_ref[rows, :] + gtf_ref[0] * moe
        ms = jnp.mean(x2 * x2, axis=-1, keepdims=True)
        o_ref[rows, :] = x2 * lax.rsqrt(ms + EPS) * fw_ref[...]

    @pl.when(step == 0)
    def _():
        gather(dcur_ref, 0, 0)

    gather(dcur_ref, 1, 1)
    finish(0, slice(0, half))

    @pl.when(step + 1 < pl.num_programs(0))
    def _():
        gather(dnext_ref, 0, 0)

    finish(1, slice(half, 2 * half))


def _combine(dest_flat, yb_tiles, x1, gates, gt_f, final_w, seq):
    n, d = x1.shape
    rows = min(256, seq)
    half = rows // 2
    steps = n // rows
    tiles_per_batch = seq // rows
    dest3 = dest_flat.reshape(steps, 1, rows * TOP_K)
    dest_blk = lambda idx: pl.BlockSpec((1, 1, rows * TOP_K), idx, memory_space=pltpu.SMEM)
    return pl.pallas_call(
        _combine_body,
        grid=(steps,),
        in_specs=[dest_blk(lambda i: (i, 0, 0)),
                  dest_blk(lambda i: (jnp.minimum(i + 1, steps - 1), 0, 0)),
                  pl.BlockSpec(memory_space=pl.ANY),
                  pl.BlockSpec((rows, d), lambda i: (i, 0)),
                  pl.BlockSpec((rows, LANES), lambda i: (i, 0)),
                  pl.BlockSpec((1, 1, d), lambda i: (i // tiles_per_batch, 0, 0)),
                  pl.BlockSpec((1, d), lambda i: (0, 0))],
        out_specs=pl.BlockSpec((rows, d), lambda i: (i, 0)),
        out_shape=jax.ShapeDtypeStruct((n, d), F32),
        scratch_shapes=[pltpu.VMEM((2, TOP_K, half * SUBLANES, LANES), F32),
                        pltpu.SemaphoreType.DMA((2,))],
        compiler_params=_params("arbitrary"),
        name="combine",
    )(dest3, dest3, yb_tiles, x1, gates, gt_f, final_w.reshape(1, d))


def _layer(x2d, c_mod, bsz, seq, norm1_w, w_in, conv_qkv_w, a_log, dt_bias, onorm_w, w_up_a,
           conv_sc_w, w_out_sc, w_o, norm2_w, router_w, router_b, moe_w1, moe_b1, moe_w2, moe_b2):
    n, d = x2d.shape
    dn = N_HEADS * HEAD_DIM
    sh_m, sc_m, gt_m, sh_f, sc_f, gt_f = (m.reshape(bsz, 1, d) for m in jnp.split(c_mod, 6, axis=-1))

    o_small, o_sc = 4 * dn, 4 * dn + 4 * N_HEADS
    w_main = jnp.concatenate([w_in[:, :o_small], w_in[:, o_sc:]], axis=1).astype(BF16)
    w_small = jnp.pad(w_in[:, o_small:o_sc], ((0, 0), (0, LANES - 4 * N_HEADS))).astype(BF16)
    p_main, p_small = _in_projection(x2d, norm1_w, sc_m, sh_m, w_main, w_small, seq)

    y_sc = _sc_conv(p_main, conv_sc_w, seq, first_col_block=4 * dn // d)
    local = _delta_local(p_main, p_small, conv_qkv_w, a_log, dt_bias, seq)
    o_f, o_b = _delta_scan(local, bsz, seq)
    x1, h2_tiles, logits = _mixer_out(o_f, o_b, p_main, y_sc, x2d, onorm_w, w_up_a, w_out_sc, w_o,
                                      gt_m, norm2_w, sc_f, sh_f, router_w, router_b, seq,
                                      z_blk=3 * dn // d, ga_blk=(4 * dn + 3 * d) // d)

    er, gates, counts = _route(logits)
    m_rows = n * TOP_K
    n_blocks = m_rows // MOE_ROWS + N_EXPERTS
    cnt = counts[0, :N_EXPERTS].astype(jnp.int32)
    padded = (cnt + MOE_ROWS - 1) // MOE_ROWS * MOE_ROWS
    pad_end = jnp.cumsum(padded)
    pad_start = pad_end - padded
    blk_row0 = jnp.arange(n_blocks, dtype=jnp.int32) * MOE_ROWS
    blk_exp = jnp.minimum(jnp.sum(pad_end[None, :] <= blk_row0[:, None], axis=1),
                          N_EXPERTS - 1).astype(jnp.int32)
    n_used = (pad_end[-1:] // MOE_ROWS).astype(jnp.int32)
    blk_first = jnp.concatenate([jnp.ones((1,), jnp.int32),
                                 (blk_exp[1:] != blk_exp[:-1]).astype(jnp.int32)])
    ends_padded = jnp.any((pad_end[None, :] == (blk_row0 + MOE_ROWS)[:, None])
                          & (padded != cnt)[None, :], axis=1)
    zero_blocks = (ends_padded | (blk_row0 >= pad_end[-1])).astype(jnp.int32)
    ps_row = jnp.broadcast_to(
        jnp.pad(pad_start.astype(F32), (0, LANES - N_EXPERTS)).reshape(1, LANES), (SUBLANES, LANES))
    dest = _dest_rows(er, ps_row)[:, :TOP_K].reshape(m_rows)

    xs_tiles = _dispatch(dest, zero_blocks, h2_tiles, n, n_blocks * MOE_ROWS)
    yb_tiles = _experts(xs_tiles, blk_exp, n_used, blk_first, moe_w1, moe_b1, moe_w2, moe_b2)
    return dest, yb_tiles, x1, gates, gt_f


def kernel(x, c, ada_w, ada_b, norm1_w, w_in, conv_qkv_w, a_log, dt_bias, onorm_w, w_up_a, conv_sc_w, w_out_sc, w_o, norm2_w, router_w, router_b, moe_w1, moe_b1, moe_w2, moe_b2, final_norm_w):
    bsz, seq, d = x.shape
    depth = ada_w.shape[0]
    assert depth == 1, "the combine stage fuses the final norm, which needs a single layer"
    x2d = x.reshape(bsz * seq, d)
    c_mod = _ada_mod(c, ada_w[0], ada_b[0])
    dest, yb_tiles, x1, gates, gt_f = _layer(
        x2d, c_mod, bsz, seq, norm1_w[0], w_in[0], conv_qkv_w[0], a_log[0], dt_bias[0], onorm_w[0],
        w_up_a[0], conv_sc_w[0], w_out_sc[0], w_o[0], norm2_w[0], router_w[0], router_b[0],
        moe_w1[0], moe_b1[0], moe_w2[0], moe_b2[0])
    out = _combine(dest, yb_tiles, x1, gates, gt_f, final_norm_w, seq)
    return out.reshape(bsz, seq, d)
```

```python
import functools

import jax
import jax.numpy as jnp
from jax import lax
from jax.experimental import pallas as pl
from jax.experimental.pallas import tpu as pltpu

F32 = jnp.float32
BF16 = jnp.bfloat16

EPS = 1e-6
N_HEADS = 8
HEAD_DIM = 128
CHUNK = 64
N_EXPERTS = 32
TOP_K = 4
SWIGLU_LIMIT = 7.0
SWIGLU_ALPHA = 1.702
LANES = 128
SUBLANES = 8
MOE_ROWS = 512
SCAN_CHUNKS = 4
VMEM_LIMIT = 56 * 1024 * 1024


def _sigmoid(x):
    return 1.0 / (1.0 + jnp.exp(-x))


def _softplus(x):
    return jnp.maximum(x, 0.0) + jnp.log(1.0 + jnp.exp(-jnp.abs(x)))


def _dot(a, b):
    return jnp.dot(a, b, preferred_element_type=F32)


def _bmm(a, b):
    return lax.dot_general(a, b, (((2,), (1,)), ((0,), (0,))), preferred_element_type=F32)


def _bmm_nt(a, b):
    return lax.dot_general(a, b, (((2,), (2,)), ((0,), (0,))), preferred_element_type=F32)


def _params(*sem, vmem=VMEM_LIMIT):
    return pltpu.CompilerParams(dimension_semantics=sem, vmem_limit_bytes=vmem)


def _rows_to_tiles(ref, value):
    n_rows = value.shape[0]
    for s in range(SUBLANES):
        ref[pl.ds(s, n_rows, stride=SUBLANES), :] = value[:, s * LANES:(s + 1) * LANES]


def _tiles_to_rows(ref, n_rows):
    return jnp.concatenate(
        [ref[pl.ds(s, n_rows, stride=SUBLANES), :] for s in range(SUBLANES)], axis=1)


def _ada_body(c_ref, w_ref, b_ref, o_ref):
    c = c_ref[...]
    ca = c * _sigmoid(c)
    o_ref[...] = _dot(ca.astype(BF16), w_ref[...].astype(BF16)) + b_ref[...]


def _ada_mod(c, ada_w, ada_b):
    bsz, d = c.shape
    n_out = ada_w.shape[1]
    cp = jnp.pad(c, ((0, SUBLANES - bsz), (0, 0)))
    out = pl.pallas_call(
        _ada_body,
        grid=(n_out // d,),
        in_specs=[pl.BlockSpec((SUBLANES, d), lambda j: (0, 0)),
                  pl.BlockSpec((d, d), lambda j: (0, j)),
                  pl.BlockSpec((1, d), lambda j: (0, j))],
        out_specs=pl.BlockSpec((SUBLANES, d), lambda j: (0, j)),
        out_shape=jax.ShapeDtypeStruct((SUBLANES, n_out), F32),
        compiler_params=_params("parallel"),
        name="ada_mod",
    )(cp, ada_w, ada_b.reshape(1, n_out))
    return out[:bsz]


def _inproj_body(x_ref, nw_ref, sc_ref, sh_ref, wa_ref, wb_ref, ws_ref, o_ref, os_ref, h_scr, *,
                 n_a):
    j = pl.program_id(1)

    @pl.when(j == 0)
    def _():
        x = x_ref[...]
        ms = jnp.mean(x * x, axis=-1, keepdims=True)
        xn = x * lax.rsqrt(ms + EPS) * nw_ref[...]
        h = (xn * (1.0 + sc_ref[0]) + sh_ref[0]).astype(BF16)
        h_scr[...] = h
        os_ref[...] = _dot(h, ws_ref[...])

    @pl.when(j < n_a)
    def _():
        o_ref[...] = _dot(h_scr[...], wa_ref[...])

    @pl.when(j >= n_a)
    def _():
        o_ref[...] = _dot(h_scr[...], wb_ref[...])


def _in_projection(x2d, norm_w, sc_m, sh_m, w_a, w_b, w_small, seq):
    n, d = x2d.shape
    tm = min(2048, seq)
    tn = 1024
    n_a = w_a.shape[1] // tn
    width = w_a.shape[1] + w_b.shape[1]
    tiles_per_batch = seq // tm
    return pl.pallas_call(
        functools.partial(_inproj_body, n_a=n_a),
        grid=(n // tm, width // tn),
        in_specs=[pl.BlockSpec((tm, d), lambda i, j: (i, 0)),
                  pl.BlockSpec((1, d), lambda i, j: (0, 0)),
                  pl.BlockSpec((1, 1, d), lambda i, j: (i // tiles_per_batch, 0, 0)),
                  pl.BlockSpec((1, 1, d), lambda i, j: (i // tiles_per_batch, 0, 0)),
                  pl.BlockSpec((d, tn), lambda i, j: (0, jnp.minimum(j, n_a - 1))),
                  pl.BlockSpec((d, tn), lambda i, j: (0, jnp.maximum(j - n_a, 0))),
                  pl.BlockSpec((d, LANES), lambda i, j: (0, 0))],
        out_specs=[pl.BlockSpec((tm, tn), lambda i, j: (i, j)),
                   pl.BlockSpec((tm, LANES), lambda i, j: (i, 0))],
        out_shape=[jax.ShapeDtypeStruct((n, width), F32),
                   jax.ShapeDtypeStruct((n, LANES), F32)],
        scratch_shapes=[pltpu.VMEM((tm, d), BF16)],
        compiler_params=_params("parallel", "arbitrary"),
        name="in_proj",
    )(x2d, norm_w.reshape(1, d), sc_m, sh_m, w_a, w_b, w_small)


def _conv_staged(scr, prev, cur, nxt, cw, width):
    rows = cur.shape[0]
    scr[0:SUBLANES, :] = prev
    scr[SUBLANES:SUBLANES + rows, :] = cur
    scr[SUBLANES + rows:, :] = nxt
    pad = width // 2
    acc = scr[pl.ds(SUBLANES - pad, rows), :] * cw[0:1]
    for w in range(1, width):
        acc = acc + scr[pl.ds(SUBLANES - pad + w, rows), :] * cw[w:w + 1]
    return acc


def _halo_specs(rows, cols, col_block, n_rows):
    per = rows // SUBLANES
    last = n_rows // SUBLANES - 1
    cur = pl.BlockSpec((rows, cols), lambda i, j: (i, col_block(j)))
    prev = pl.BlockSpec((SUBLANES, cols), lambda i, j: (jnp.maximum(i * per - 1, 0), col_block(j)))
    nxt = pl.BlockSpec((SUBLANES, cols), lambda i, j: (jnp.minimum((i + 1) * per, last), col_block(j)))
    return cur, prev, nxt


def _sc_conv_body(b_ref, c_ref, cp_ref, cn_ref, u_ref, up_ref, un_ref, cw_ref, o_ref, *,
                  tiles_per_batch, width):
    t_in_b = pl.program_id(0) % tiles_per_batch
    rows = c_ref.shape[0]
    length = rows + 2 * SUBLANES
    prev = jnp.where(t_in_b == 0, 0.0, cp_ref[...] * up_ref[...])
    nxt = jnp.where(t_in_b == tiles_per_batch - 1, 0.0, cn_ref[...] * un_ref[...])
    ext = jnp.concatenate([prev, c_ref[...] * u_ref[...], nxt], axis=0)
    cw = cw_ref[...]
    pad = width // 2
    acc = ext * cw[pad:pad + 1]
    for w in range(width):
        if w != pad:
            acc = acc + pltpu.roll(ext, (pad - w) % length, axis=0) * cw[w:w + 1]
    o_ref[...] = (b_ref[...] * acc[SUBLANES:SUBLANES + rows]).astype(BF16)


def _sc_conv(p_main, conv_w, seq, first_col_block):
    n = p_main.shape[0]
    width, ch = conv_w.shape
    rows = min(512, seq)
    cw = jnp.pad(conv_w, ((0, SUBLANES - width), (0, 0)))
    b_spec = pl.BlockSpec((rows, ch), lambda i, j: (i, first_col_block))
    c_specs = _halo_specs(rows, ch, lambda j: first_col_block + 1, n)
    u_specs = _halo_specs(rows, ch, lambda j: first_col_block + 2, n)
    body = functools.partial(_sc_conv_body, tiles_per_batch=seq // rows, width=width)
    return pl.pallas_call(
        body,
        grid=(n // rows, 1),
        in_specs=[b_spec, *c_specs, *u_specs, pl.BlockSpec((SUBLANES, ch), lambda i, j: (0, 0))],
        out_specs=pl.BlockSpec((rows, ch), lambda i, j: (i, 0)),
        out_shape=jax.ShapeDtypeStruct((n, ch), BF16),
        compiler_params=_params("parallel", "arbitrary"),
        name="sc_conv",
    )(p_main, p_main, p_main, p_main, p_main, p_main, p_main, cw)


def _block_diag(x, lo_half):
    return jnp.concatenate([jnp.where(lo_half, x, 0.0), jnp.where(lo_half, 0.0, x)], axis=1)


def _delta_local_body(qc_ref, qp_ref, qn_ref, qw_ref, kc_ref, kp_ref, kn_ref, kw_ref,
                      vc_ref, vp_ref, vn_ref, vw_ref, s_ref, alog_ref, dtb_ref,
                      uf_ref, ub_ref, wf_ref, wb_ref, qf_ref, qb_ref, a_ref, kt_ref, ef_ref, eb_ref,
                      beta_scr, g_scr, gl_scr, gt_scr, gtr_scr, cq_scr, ck_scr, cv_scr, *,
                      tiles_per_batch, width):
    h = pl.program_id(1)
    rows = qc_ref.shape[0]
    n_chunks = rows // CHUNK
    lane = lax.broadcasted_iota(jnp.int32, (rows, LANES), 1)

    @pl.when(h == 0)
    def _():
        s = s_ref[...]
        beta_scr[...] = _sigmoid(s)
        g = -jnp.exp(alog_ref[...]) * _softplus(s + dtb_ref[...])
        pos = lax.broadcasted_iota(jnp.int32, (rows, LANES), 0) & (CHUNK - 1)
        gp = g
        gs = g
        step = 1
        while step < CHUNK:
            gp = gp + jnp.where(pos >= step, pltpu.roll(gp, step, axis=0), 0.0)
            gs = gs + jnp.where(pos < CHUNK - step, pltpu.roll(gs, rows - step, axis=0), 0.0)
            step *= 2
        gc = jnp.where(lane >= 3 * N_HEADS, gs, gp)
        g_scr[...] = gc
        gl_scr[...] = gp + gs - g
        for c2 in range(rows // LANES):
            gt = gc[c2 * LANES:(c2 + 1) * LANES, :].T
            gt_scr[c2] = gt
            gtr_scr[c2] = pltpu.roll(gt, CHUNK, axis=1)

    t_in_b = pl.program_id(0) % tiles_per_batch

    def conv_in(cur_ref, prev_ref, next_ref, w_ref, scr, normed):
        prev = jnp.where(t_in_b == 0, 0.0, prev_ref[...])
        nxt = jnp.where(t_in_b == tiles_per_batch - 1, 0.0, next_ref[...])
        y = _conv_staged(scr, prev, cur_ref[...], nxt, w_ref[...], width)
        y = y * _sigmoid(y)
        if normed:
            y = y * lax.rsqrt(jnp.sum(y * y, axis=-1, keepdims=True) + EPS)
        return y

    q2 = conv_in(qc_ref, qp_ref, qn_ref, qw_ref, cq_scr, True) * (HEAD_DIM ** -0.5)
    k2 = conv_in(kc_ref, kp_ref, kn_ref, kw_ref, ck_scr, True)
    v2 = conv_in(vc_ref, vp_ref, vn_ref, vw_ref, cv_scr, False)

    def pick(ref, idx):
        return jnp.sum(jnp.where(lane == idx, ref[...], 0.0), axis=-1, keepdims=True)

    i_f, i_b = 2 * N_HEADS + h, 3 * N_HEADS + h
    beta_f, beta_b = pick(beta_scr, h), pick(beta_scr, N_HEADS + h)
    g_f, g_b = pick(g_scr, i_f), pick(g_scr, i_b)
    gl_f, gl_b = pick(gl_scr, i_f), pick(gl_scr, i_b)

    kb_f, kb_b = k2 * beta_f, k2 * beta_b
    eg_f, eg_b = jnp.exp(g_f), jnp.exp(g_b)
    to3 = lambda a: a.reshape(n_chunks, CHUNK, a.shape[-1])
    bf3 = lambda a: to3(a.astype(BF16))

    k3 = bf3(k2)
    zero3 = jnp.zeros_like(k3)
    lhs = jnp.concatenate([jnp.concatenate([bf3(kb_f), bf3(kb_b)], axis=2),
                           jnp.concatenate([bf3(q2), bf3(q2)], axis=2)], axis=1)
    rhs_t = jnp.concatenate([jnp.concatenate([k3, zero3], axis=2),
                             jnp.concatenate([zero3, k3], axis=2)], axis=1)
    m1 = _bmm_nt(lhs, rhs_t)

    ii = lax.broadcasted_iota(jnp.int32, (CHUNK, LANES), 0)
    l2 = lax.broadcasted_iota(jnp.int32, (CHUNK, LANES), 1)
    jj = l2 & (CHUNK - 1)
    lo_half = l2 < CHUNK
    ahead = jnp.where(lo_half, ii - jj, jj - ii)
    incl = ahead >= 0
    strict = ahead > 0
    eye = (ii == jj).astype(F32)

    g_cols = jnp.where(lane < CHUNK, g_f, g_b)
    lrow = lax.broadcasted_iota(jnp.int32, (1, LANES), 1) < CHUNK
    diffs = []
    for c in range(n_chunks):
        rf = gt_scr[c // 2, pl.ds(i_f, 1), :]
        rb = gt_scr[c // 2, pl.ds(i_b, 1), :]
        if c % 2 == 0:
            g_row = jnp.where(lrow, rf, gtr_scr[c // 2, pl.ds(i_b, 1), :])
        else:
            g_row = jnp.where(lrow, gtr_scr[c // 2, pl.ds(i_f, 1), :], rb)
        diffs.append(g_cols[c * CHUNK:(c + 1) * CHUNK] - g_row)
    decay = jnp.exp(jnp.where(incl, jnp.stack(diffs), -jnp.inf))

    nm = -jnp.where(strict, m1[:, :CHUNK] * decay, 0.0)
    a_ref[...] = (m1[:, CHUNK:] * decay).astype(BF16).reshape(rows, LANES)

    p = eye + nm
    npow = _bmm(nm.astype(BF16), _block_diag(nm, lo_half).astype(BF16))
    n_iter = CHUNK.bit_length() - 2
    for it in range(n_iter):
        bd = _block_diag(npow, lo_half).astype(BF16)
        if it + 1 < n_iter:
            x = _bmm(jnp.concatenate([p, npow], axis=1).astype(BF16), bd)
            p = p + x[:, :CHUNK]
            npow = x[:, CHUNK:]
        else:
            p = p + _bmm(p.astype(BF16), bd)

    rhs = jnp.concatenate([jnp.concatenate([bf3(v2 * beta_f), bf3(kb_f * eg_f)], axis=2),
                           jnp.concatenate([bf3(v2 * beta_b), bf3(kb_b * eg_b)], axis=2)], axis=1)
    uw = _bmm(_block_diag(p, lo_half).astype(BF16), rhs)
    uf_ref[...] = uw[:, :CHUNK, :HEAD_DIM].reshape(rows, HEAD_DIM)
    ub_ref[...] = uw[:, CHUNK:, :HEAD_DIM].reshape(rows, HEAD_DIM)
    wf_ref[...] = uw[:, :CHUNK, HEAD_DIM:].astype(BF16).reshape(rows, HEAD_DIM)
    wb_ref[...] = uw[:, CHUNK:, HEAD_DIM:].astype(BF16).reshape(rows, HEAD_DIM)
    qf_ref[...] = (q2 * eg_f).astype(BF16)
    qb_ref[...] = (q2 * eg_b).astype(BF16)

    kd_f = to3(k2 * jnp.exp(gl_f - g_f))
    kd_b = to3(k2 * jnp.exp(gl_b - g_b))
    e_f = to3(jnp.broadcast_to(jnp.exp(gl_f), (rows, LANES)))
    e_b = to3(jnp.broadcast_to(jnp.exp(gl_b), (rows, LANES)))
    ef_ref[...] = e_f[:, :SUBLANES, :].reshape(n_chunks * SUBLANES, LANES)
    eb_ref[...] = e_b[:, :SUBLANES, :].reshape(n_chunks * SUBLANES, LANES)
    for c in range(n_chunks):
        kt_ref[c * 2 * CHUNK:(c + 1) * 2 * CHUNK, :] = (
            jnp.concatenate([kd_f[c], kd_b[c]], axis=0).T.astype(BF16))


def _delta_local(p_main, p_small, conv_w, a_log, dt_bias, seq):
    n = p_main.shape[0]
    width = conv_w.shape[0]
    cols = N_HEADS * HEAD_DIM
    rows = min(2048, seq)
    cw = jnp.pad(conv_w, ((0, SUBLANES - width), (0, 0)))
    alog_row = jnp.pad(a_log.reshape(1, -1), ((0, 0), (2 * N_HEADS, LANES - 4 * N_HEADS)))
    dtb_row = jnp.pad(dt_bias.reshape(1, -1), ((0, 0), (2 * N_HEADS, LANES - 4 * N_HEADS)))

    def qkv_specs(off):
        col = lambda h: off + h
        return [*_halo_specs(rows, HEAD_DIM, col, n),
                pl.BlockSpec((SUBLANES, HEAD_DIM), lambda i, h: (0, off + h))]

    row_spec = pl.BlockSpec((1, LANES), lambda i, h: (0, 0))
    out_blk = pl.BlockSpec((rows, HEAD_DIM), lambda i, h: (i, h))
    f32_out = jax.ShapeDtypeStruct((n, cols), F32)
    bf_out = jax.ShapeDtypeStruct((n, cols), BF16)
    eg_out = jax.ShapeDtypeStruct((n // SUBLANES, cols), F32)
    eg_blk = pl.BlockSpec((rows // SUBLANES, HEAD_DIM), lambda i, h: (i, h))
    tile = pltpu.VMEM((rows, LANES), F32)
    tposed = pltpu.VMEM((rows // LANES, LANES, LANES), F32)
    staged = pltpu.VMEM((rows + 2 * SUBLANES, HEAD_DIM), F32)
    body = functools.partial(_delta_local_body, tiles_per_batch=seq // rows, width=width)
    qkv_in = [p_main, p_main, p_main, cw]
    return pl.pallas_call(
        body,
        grid=(n // rows, N_HEADS),
        in_specs=[*qkv_specs(0), *qkv_specs(N_HEADS), *qkv_specs(2 * N_HEADS),
                  pl.BlockSpec((rows, LANES), lambda i, h: (i, 0)), row_spec, row_spec],
        out_specs=[out_blk, out_blk, out_blk, out_blk, out_blk, out_blk, out_blk,
                   pl.BlockSpec((2 * rows, HEAD_DIM), lambda i, h: (i, h)), eg_blk, eg_blk],
        out_shape=[f32_out, f32_out, bf_out, bf_out, bf_out, bf_out, bf_out,
                   jax.ShapeDtypeStruct((2 * n, cols), BF16), eg_out, eg_out],
        scratch_shapes=[tile, tile, tile, tposed, tposed, staged, staged, staged],
        compiler_params=_params("parallel", "arbitrary"),
        name="delta_local",
    )(*qkv_in, *qkv_in, *qkv_in, p_small, alog_row, dtb_row)


def _delta_scan_body(uf_ref, wf_ref, qf_ref, af_ref, ktf_ref, ef_ref,
                     ub_ref, wb_ref, qb_ref, ab_ref, ktb_ref, eb_ref,
                     of_ref, ob_ref, s_scr, *, sub_chunks):
    @pl.when(pl.program_id(1) == 0)
    def _():
        s_scr[...] = jnp.zeros_like(s_scr)

    heads = [slice(h * HEAD_DIM, (h + 1) * HEAD_DIM) for h in range(N_HEADS)]
    state = s_scr[...]
    for j in range(sub_chunks):
        jb = sub_chunks - 1 - j

        def per_head(ref_f, ref_b, unit, rows=None, half=False):
            rows = unit if rows is None else rows
            lo, hi = ((slice(0, CHUNK), slice(CHUNK, 2 * CHUNK)) if half
                      else (slice(None), slice(None)))
            return jnp.stack([ref_f[j * unit:j * unit + rows, hs][:, lo] for hs in heads]
                             + [ref_b[jb * unit:jb * unit + rows, hs][:, hi] for hs in heads])

        wq = jnp.concatenate([per_head(wf_ref, wb_ref, CHUNK), per_head(qf_ref, qb_ref, CHUNK)],
                             axis=1)
        x = _bmm(wq, state.astype(BF16))
        v_new = (per_head(uf_ref, ub_ref, CHUNK) - x[:, :CHUNK]).astype(BF16)
        lhs = jnp.concatenate([per_head(af_ref, ab_ref, CHUNK, half=True),
                               per_head(ktf_ref, ktb_ref, 2 * CHUNK, half=True)], axis=1)
        y = _bmm(lhs, v_new)
        out = x[:, CHUNK:] + y[:, :CHUNK]
        for h, hs in enumerate(heads):
            of_ref[j * CHUNK:(j + 1) * CHUNK, hs] = out[h]
            ob_ref[jb * CHUNK:(jb + 1) * CHUNK, hs] = out[N_HEADS + h]
        state = state * per_head(ef_ref, eb_ref, SUBLANES, rows=1) + y[:, CHUNK:]
    s_scr[...] = state


def _delta_scan(local_out, bsz, seq):
    uf, ub, wf, wb, qf, qb, a, kt, ef, eb = local_out
    n, cols = uf.shape
    n_chunks = seq // CHUNK
    sub = min(SCAN_CHUNKS, n_chunks)
    steps = n_chunks // sub
    fwd = lambda b, c: (b * steps + c, 0)
    bwd = lambda b, c: (b * steps + (steps - 1 - c), 0)
    blk = lambda unit, idx: pl.BlockSpec((sub * unit, cols), idx)
    specs = lambda idx: [blk(CHUNK, idx), blk(CHUNK, idx), blk(CHUNK, idx), blk(CHUNK, idx),
                         blk(2 * CHUNK, idx), blk(SUBLANES, idx)]
    return pl.pallas_call(
        functools.partial(_delta_scan_body, sub_chunks=sub),
        grid=(bsz, steps),
        in_specs=specs(fwd) + specs(bwd),
        out_specs=[blk(CHUNK, fwd), blk(CHUNK, bwd)],
        out_shape=[jax.ShapeDtypeStruct((n, cols), F32), jax.ShapeDtypeStruct((n, cols), F32)],
        scratch_shapes=[pltpu.VMEM((2 * N_HEADS, HEAD_DIM, HEAD_DIM), F32)],
        compiler_params=_params("parallel", "arbitrary"),
        name="delta_scan",
    )(uf, wf, qf, a, kt, ef, ub, wb, qb, a, kt, eb)


def _mixer_out_body(of_ref, ob_ref, z_ref, yb_ref, ga_ref, gb_ref, x_ref,
                    onw_ref, wa_ref, wb_ref, wo_ref, gtm_ref, n2w_ref, scf_ref, shf_ref,
                    rwh_ref, rwl_ref, rb_ref, x1_ref, h2_ref, lg_ref, oz_scr):
    o = of_ref[...] + ob_ref[...]
    z = z_ref[...]
    onw = onw_ref[...]
    for h in range(N_HEADS):
        hs = slice(h * HEAD_DIM, (h + 1) * HEAD_DIM)
        oh = o[:, hs]
        ms = jnp.mean(oh * oh, axis=-1, keepdims=True)
        zh = z[:, hs]
        oz_scr[:, hs] = (oh * lax.rsqrt(ms + EPS) * onw * (zh * _sigmoid(zh))).astype(BF16)
    y_a = _dot(oz_scr[...], wa_ref[...])
    y_b = _dot(yb_ref[...], wb_ref[...])
    merged = _sigmoid(ga_ref[...]) * y_a + _sigmoid(gb_ref[...]) * y_b
    x1 = x_ref[...] + gtm_ref[0] * _dot(merged.astype(BF16), wo_ref[...])
    x1_ref[...] = x1
    ms = jnp.mean(x1 * x1, axis=-1, keepdims=True)
    h2 = x1 * lax.rsqrt(ms + EPS) * n2w_ref[...] * (1.0 + scf_ref[0]) + shf_ref[0]
    _rows_to_tiles(h2_ref, h2)
    h_hi = h2.astype(BF16)
    h_lo = (h2 - h_hi.astype(F32)).astype(BF16)
    rwh = rwh_ref[...]
    lg_ref[...] = (_dot(h_hi, rwh) + (_dot(h_lo, rwh) + _dot(h_hi, rwl_ref[...]))) + rb_ref[...]


def _mixer_out(o_f, o_b, p_main, y_sc, x2d, onorm_w, w_up_a, w_out_sc, w_o, gt_m, norm2_w,
               sc_f, sh_f, router_w, router_b, seq, z_blk, ga_blk):
    n, d = x2d.shape
    tm = min(512, seq)
    tiles_per_batch = seq // tm
    row = lambda i: (i, 0)
    const = lambda i: (0, 0)
    per_b = lambda i: (i // tiles_per_batch, 0, 0)
    weight = pl.BlockSpec((d, d), const, pipeline_mode=pl.Buffered(1))
    rw = jnp.pad(router_w, ((0, 0), (0, LANES - N_EXPERTS)))
    rw_hi = rw.astype(BF16)
    rw_lo = (rw - rw_hi.astype(F32)).astype(BF16)
    rb = jnp.pad(router_b.reshape(1, -1), ((0, 0), (0, LANES - N_EXPERTS)))
    return pl.pallas_call(
        _mixer_out_body,
        grid=(n // tm,),
        in_specs=[pl.BlockSpec((tm, d), row), pl.BlockSpec((tm, d), row),
                  pl.BlockSpec((tm, d), lambda i: (i, z_blk)), pl.BlockSpec((tm, d), row),
                  pl.BlockSpec((tm, d), lambda i: (i, ga_blk)),
                  pl.BlockSpec((tm, d), lambda i: (i, ga_blk + 1)), pl.BlockSpec((tm, d), row),
                  pl.BlockSpec((1, HEAD_DIM), const), weight, weight, weight,
                  pl.BlockSpec((1, 1, d), per_b), pl.BlockSpec((1, d), const),
                  pl.BlockSpec((1, 1, d), per_b), pl.BlockSpec((1, 1, d), per_b),
                  pl.BlockSpec((d, LANES), const), pl.BlockSpec((d, LANES), const),
                  pl.BlockSpec((1, LANES), const)],
        out_specs=[pl.BlockSpec((tm, d), row), pl.BlockSpec((tm * SUBLANES, LANES), row),
                   pl.BlockSpec((tm, LANES), row)],
        out_shape=[jax.ShapeDtypeStruct((n, d), F32),
                   jax.ShapeDtypeStruct((n * SUBLANES, LANES), F32),
                   jax.ShapeDtypeStruct((n, LANES), F32)],
        scratch_shapes=[pltpu.VMEM((tm, d), BF16)],
        compiler_params=_params("parallel"),
        name="mixer_out",
    )(o_f, o_b, p_main, y_sc, p_main, p_main, x2d, onorm_w.reshape(1, HEAD_DIM),
      w_up_a.astype(BF16), w_out_sc.astype(BF16), w_o.astype(BF16), gt_m,
      norm2_w.reshape(1, d), sc_f, sh_f, rw_hi, rw_lo, rb)


def _route_body(lg_ref, er_ref, gate_ref, cnt_ref, carry_scr):
    @pl.when(pl.program_id(0) == 0)
    def _():
        carry_scr[...] = jnp.zeros_like(carry_scr)

    rows = lg_ref.shape[0]
    lane = lax.broadcasted_iota(jnp.int32, (rows, LANES), 1)
    lanef = lane.astype(F32)
    logits = jnp.where(lane < N_EXPERTS, lg_ref[...], -jnp.inf)
    vals, idxs = [], []
    sel = jnp.zeros((rows, LANES), F32)
    for _ in range(TOP_K):
        m = jnp.max(logits, axis=-1, keepdims=True)
        idx = jnp.min(jnp.where(logits == m, lanef, float(LANES)), axis=-1, keepdims=True)
        hit = lanef == idx
        vals.append(m)
        idxs.append(idx)
        sel = sel + hit.astype(F32)
        logits = jnp.where(hit, -jnp.inf, logits)
    exps = [jnp.exp(v - vals[0]) for v in vals]
    total = exps[0]
    for e in exps[1:]:
        total = total + e
    r = lax.broadcasted_iota(jnp.int32, (rows, rows), 0)
    c = lax.broadcasted_iota(jnp.int32, (rows, rows), 1)
    earlier = (r > c).astype(BF16)
    carry = carry_scr[0:1, :]
    before = _dot(earlier, sel.astype(BF16)) + carry
    er = jnp.zeros((rows, LANES), F32)
    gates = jnp.zeros((rows, LANES), F32)
    for k in range(TOP_K):
        rank = jnp.sum(jnp.where(lanef == idxs[k], before, 0.0), axis=-1, keepdims=True)
        er = jnp.where(lane == k, idxs[k], er)
        er = jnp.where(lane == TOP_K + k, rank, er)
        gates = jnp.where(lane == k, exps[k] / total, gates)
    er_ref[...] = er
    gate_ref[...] = gates
    new_carry = carry + jnp.sum(sel, axis=0, keepdims=True)
    carry_scr[...] = jnp.broadcast_to(new_carry, carry_scr.shape)
    cnt_ref[...] = jnp.broadcast_to(new_carry, cnt_ref.shape)


def _route(logits):
    n = logits.shape[0]
    rows = min(512, n)
    tile = pl.BlockSpec((rows, LANES), lambda i: (i, 0))
    return pl.pallas_call(
        _route_body,
        grid=(n // rows,),
        in_specs=[tile],
        out_specs=[tile, tile, pl.BlockSpec((SUBLANES, LANES), lambda i: (0, 0))],
        out_shape=[jax.ShapeDtypeStruct((n, LANES), F32), jax.ShapeDtypeStruct((n, LANES), F32),
                   jax.ShapeDtypeStruct((SUBLANES, LANES), F32)],
        scratch_shapes=[pltpu.VMEM((SUBLANES, LANES), F32)],
        compiler_params=_params("arbitrary"),
        name="route",
    )(logits)


def _dest_body(er_ref, ps_ref, o_ref):
    rows = er_ref.shape[0]
    lane = lax.broadcasted_iota(jnp.int32, (rows, LANES), 1)
    lanef = lane.astype(F32)
    er = er_ref[...]
    starts = ps_ref[0:1, :]
    out = jnp.zeros((rows, LANES), F32)
    for k in range(TOP_K):
        e_k = jnp.sum(jnp.where(lane == k, er, 0.0), axis=-1, keepdims=True)
        r_k = jnp.sum(jnp.where(lane == TOP_K + k, er, 0.0), axis=-1, keepdims=True)
        base = jnp.sum(jnp.where(lanef == e_k, starts, 0.0), axis=-1, keepdims=True)
        out = jnp.where(lane == k, base + r_k, out)
    o_ref[...] = out.astype(jnp.int32)


def _dest_rows(er, pad_start_row):
    n = er.shape[0]
    rows = min(1024, n)
    tile = pl.BlockSpec((rows, LANES), lambda i: (i, 0))
    return pl.pallas_call(
        _dest_body,
        grid=(n // rows,),
        in_specs=[tile, pl.BlockSpec((SUBLANES, LANES), lambda i: (0, 0))],
        out_specs=tile,
        out_shape=jax.ShapeDtypeStruct((n, LANES), jnp.int32),
        compiler_params=_params("parallel"),
        name="dest_rows",
    )(er, pad_start_row)


def _tile_at(ref, row):
    return ref.at[pl.ds(pl.multiple_of(row * SUBLANES, SUBLANES), SUBLANES)]


def _dispatch_body(zero_ref, dest_ref, h_ref, xs_hbm, zbuf, sem, zsem):
    n_tok = dest_ref.shape[-1] // TOP_K
    blk_tiles = zbuf.shape[0]

    @pl.when(pl.program_id(0) == 0)
    def _():
        zbuf[...] = jnp.zeros_like(zbuf)

        def block_copy(b):
            return pltpu.make_async_copy(
                zbuf, xs_hbm.at[pl.ds(pl.multiple_of(b * blk_tiles, blk_tiles), blk_tiles)], zsem)

        def fill(b, carry):
            @pl.when(zero_ref[b] != 0)
            def _():
                block_copy(b).start()
            return carry

        def fill_done(b, carry):
            @pl.when(zero_ref[b] != 0)
            def _():
                block_copy(b).wait()
            return carry

        lax.fori_loop(0, zero_ref.shape[0], fill, 0)
        lax.fori_loop(0, zero_ref.shape[0], fill_done, 0)

    def issue(t, carry):
        for k in range(TOP_K):
            pltpu.make_async_copy(_tile_at(h_ref, t), _tile_at(xs_hbm, dest_ref[0, 0, t * TOP_K + k]),
                                  sem).start(priority=k % 2)
        return carry

    lax.fori_loop(0, n_tok, issue, 0)
    for _ in range(TOP_K):
        pltpu.make_async_copy(h_ref, xs_hbm.at[pl.ds(0, h_ref.shape[0])], sem).wait()


def _dispatch(dest_flat, zero_blocks, h2_tiles, n_tok, m_pad):
    rows = min(512, n_tok)
    dest3 = dest_flat.reshape(n_tok // rows, 1, rows * TOP_K)
    grid_spec = pltpu.PrefetchScalarGridSpec(
        num_scalar_prefetch=1,
        grid=(n_tok // rows,),
        in_specs=[pl.BlockSpec((1, 1, rows * TOP_K), lambda i, zb: (i, 0, 0),
                               memory_space=pltpu.SMEM),
                  pl.BlockSpec((rows * SUBLANES, LANES), lambda i, zb: (i, 0))],
        out_specs=pl.BlockSpec(memory_space=pl.ANY),
        scratch_shapes=[pltpu.VMEM((MOE_ROWS * SUBLANES, LANES), h2_tiles.dtype),
                        pltpu.SemaphoreType.DMA, pltpu.SemaphoreType.DMA],
    )
    return pl.pallas_call(
        _dispatch_body,
        grid_spec=grid_spec,
        out_shape=jax.ShapeDtypeStruct((m_pad * SUBLANES, LANES), h2_tiles.dtype),
        compiler_params=_params("arbitrary"),
        name="dispatch",
    )(zero_blocks, dest3, h2_tiles)


def _expert_body(be_ref, nu_ref, x_ref, w1_ref, b1_ref, w2_ref, b2_ref, o_ref):
    del be_ref
    i = pl.program_id(0)

    @pl.when(i < nu_ref[0])
    def _():
        d_ff = w2_ref.shape[1]
        x = _tiles_to_rows(x_ref, MOE_ROWS)
        gu = _dot(x.astype(BF16), w1_ref[0].astype(BF16)) + b1_ref[0]
        gate = jnp.minimum(gu[:, :d_ff], SWIGLU_LIMIT)
        up = jnp.clip(gu[:, d_ff:], -SWIGLU_LIMIT, SWIGLU_LIMIT)
        act = gate * _sigmoid(SWIGLU_ALPHA * gate) * (up + 1.0)
        _rows_to_tiles(o_ref, _dot(act.astype(BF16), w2_ref[0].astype(BF16)) + b2_ref[0])

    @pl.when(i >= nu_ref[0])
    def _():
        o_ref[...] = jnp.zeros_like(o_ref)


def _experts(xs_tiles, blk_exp, n_used, w1, b1, w2, b2):
    n_exp, d, two_ff = w1.shape
    d_ff = two_ff // 2
    m_pad = xs_tiles.shape[0] // SUBLANES
    out_tiles = pl.BlockSpec((MOE_ROWS * SUBLANES, LANES), lambda i, be, nu: (i, 0))
    in_tiles = pl.BlockSpec((MOE_ROWS * SUBLANES, LANES),
                            lambda i, be, nu: (jnp.minimum(i, nu[0] - 1), 0))
    by_expert = lambda i, be, nu: (be[i], 0, 0)
    grid_spec = pltpu.PrefetchScalarGridSpec(
        num_scalar_prefetch=2,
        grid=(m_pad // MOE_ROWS,),
        in_specs=[in_tiles,
                  pl.BlockSpec((1, d, two_ff), by_expert), pl.BlockSpec((1, 1, two_ff), by_expert),
                  pl.BlockSpec((1, d_ff, d), by_expert), pl.BlockSpec((1, 1, d), by_expert)],
        out_specs=out_tiles,
    )
    return pl.pallas_call(
        _expert_body,
        grid_spec=grid_spec,
        out_shape=jax.ShapeDtypeStruct((m_pad * SUBLANES, LANES), F32),
        compiler_params=_params("arbitrary", vmem=60 * 1024 * 1024),
        name="experts",
    )(blk_exp, n_used, xs_tiles, w1, b1.reshape(n_exp, 1, two_ff), w2, b2.reshape(n_exp, 1, d))


def _combine_body(dcur_ref, dnext_ref, y_hbm, x1_ref, gate_ref, gtf_ref, fw_ref, o_ref, buf, sems):
    step = pl.program_id(0)
    half = x1_ref.shape[0] // 2

    def row_copy(d_ref, tile, slot, t, k):
        return pltpu.make_async_copy(
            _tile_at(y_hbm, d_ref[0, 0, (tile * half + t) * TOP_K + k]),
            _tile_at(buf.at[slot, k], t), sems.at[slot])

    def gather(d_ref, tile, slot):
        def issue(t, carry):
            for k in range(TOP_K):
                row_copy(d_ref, tile, slot, t, k).start(priority=k % 2)
            return carry
        lax.fori_loop(0, half, issue, 0)

    def finish(slot, rows):
        for k in range(TOP_K):
            pltpu.make_async_copy(y_hbm.at[pl.ds(0, half * SUBLANES)], buf.at[slot, k],
                                  sems.at[slot]).wait()
        gates = gate_ref[rows, :]
        moe = gates[:, 0:1] * _tiles_to_rows(buf.at[slot, 0], half)
        for k in range(1, TOP_K):
            moe = moe + gates[:, k:k + 1] * _tiles_to_rows(buf.at[slot, k], half)
        x2 = x1_ref[rows, :] + gtf_ref[0] * moe
        ms = jnp.mean(x2 * x2, axis=-1, keepdims=True)
        o_ref[rows, :] = x2 * lax.rsqrt(ms + EPS) * fw_ref[...]

    @pl.when(step == 0)
    def _():
        gather(dcur_ref, 0, 0)

    gather(dcur_ref, 1, 1)
    finish(0, slice(0, half))

    @pl.when(step + 1 < pl.num_programs(0))
    def _():
        gather(dnext_ref, 0, 0)

    finish(1, slice(half, 2 * half))


def _combine(dest_flat, yb_tiles, x1, gates, gt_f, final_w, seq):
    n, d = x1.shape
    rows = min(256, seq)
    half = rows // 2
    steps = n // rows
    tiles_per_batch = seq // rows
    dest3 = dest_flat.reshape(steps, 1, rows * TOP_K)
    dest_blk = lambda idx: pl.BlockSpec((1, 1, rows * TOP_K), idx, memory_space=pltpu.SMEM)
    return pl.pallas_call(
        _combine_body,
        grid=(steps,),
        in_specs=[dest_blk(lambda i: (i, 0, 0)),
                  dest_blk(lambda i: (jnp.minimum(i + 1, steps - 1), 0, 0)),
                  pl.BlockSpec(memory_space=pl.ANY),
                  pl.BlockSpec((rows, d), lambda i: (i, 0)),
                  pl.BlockSpec((rows, LANES), lambda i: (i, 0)),
                  pl.BlockSpec((1, 1, d), lambda i: (i // tiles_per_batch, 0, 0)),
                  pl.BlockSpec((1, d), lambda i: (0, 0))],
        out_specs=pl.BlockSpec((rows, d), lambda i: (i, 0)),
        out_shape=jax.ShapeDtypeStruct((n, d), F32),
        scratch_shapes=[pltpu.VMEM((2, TOP_K, half * SUBLANES, LANES), F32),
                        pltpu.SemaphoreType.DMA((2,))],
        compiler_params=_params("arbitrary"),
        name="combine",
    )(dest3, dest3, yb_tiles, x1, gates, gt_f, final_w.reshape(1, d))


def _layer(x2d, c_mod, bsz, seq, norm1_w, w_in, conv_qkv_w, a_log, dt_bias, onorm_w, w_up_a,
           conv_sc_w, w_out_sc, w_o, norm2_w, router_w, router_b, moe_w1, moe_b1, moe_w2, moe_b2):
    n, d = x2d.shape
    dn = N_HEADS * HEAD_DIM
    sh_m, sc_m, gt_m, sh_f, sc_f, gt_f = (m.reshape(bsz, 1, d) for m in jnp.split(c_mod, 6, axis=-1))

    o_small, o_sc = 4 * dn, 4 * dn + 4 * N_HEADS
    w_a = w_in[:, :o_small].astype(BF16)
    w_b = w_in[:, o_sc:].astype(BF16)
    w_small = jnp.pad(w_in[:, o_small:o_sc], ((0, 0), (0, LANES - 4 * N_HEADS))).astype(BF16)
    p_main, p_small = _in_projection(x2d, norm1_w, sc_m, sh_m, w_a, w_b, w_small, seq)

    y_sc = _sc_conv(p_main, conv_sc_w, seq, first_col_block=4 * dn // d)
    local = _delta_local(p_main, p_small, conv_qkv_w, a_log, dt_bias, seq)
    o_f, o_b = _delta_scan(local, bsz, seq)
    x1, h2_tiles, logits = _mixer_out(o_f, o_b, p_main, y_sc, x2d, onorm_w, w_up_a, w_out_sc, w_o,
                                      gt_m, norm2_w, sc_f, sh_f, router_w, router_b, seq,
                                      z_blk=3 * dn // d, ga_blk=(4 * dn + 3 * d) // d)

    er, gates, counts = _route(logits)
    m_rows = n * TOP_K
    n_blocks = m_rows // MOE_ROWS + N_EXPERTS
    cnt = counts[0, :N_EXPERTS].astype(jnp.int32)
    padded = (cnt + MOE_ROWS - 1) // MOE_ROWS * MOE_ROWS
    pad_end = jnp.cumsum(padded)
    pad_start = pad_end - padded
    blk_row0 = jnp.arange(n_blocks, dtype=jnp.int32) * MOE_ROWS
    blk_exp = jnp.minimum(jnp.sum(pad_end[None, :] <= blk_row0[:, None], axis=1),
                          N_EXPERTS - 1).astype(jnp.int32)
    n_used = (pad_end[-1:] // MOE_ROWS).astype(jnp.int32)
    ends_padded = jnp.any((pad_end[None, :] == (blk_row0 + MOE_ROWS)[:, None])
                          & (padded != cnt)[None, :], axis=1)
    zero_blocks = (ends_padded | (blk_row0 >= pad_end[-1])).astype(jnp.int32)
    ps_row = jnp.broadcast_to(
        jnp.pad(pad_start.astype(F32), (0, LANES - N_EXPERTS)).reshape(1, LANES), (SUBLANES, LANES))
    dest = _dest_rows(er, ps_row)[:, :TOP_K].reshape(m_rows)

    xs_tiles = _dispatch(dest, zero_blocks, h2_tiles, n, n_blocks * MOE_ROWS)
    yb_tiles = _experts(xs_tiles, blk_exp, n_used, moe_w1, moe_b1, moe_w2, moe_b2)
    return dest, yb_tiles, x1, gates, gt_f


def kernel(x, c, ada_w, ada_b, norm1_w, w_in, conv_qkv_w, a_log, dt_bias, onorm_w, w_up_a, conv_sc_w, w_out_sc, w_o, norm2_w, router_w, router_b, moe_w1, moe_b1, moe_w2, moe_b2, final_norm_w):
    bsz, seq, d = x.shape
    depth = ada_w.shape[0]
    assert depth == 1, "the combine stage fuses the final norm, which needs a single layer"
    x2d = x.reshape(bsz * seq, d)
    c_mod = _ada_mod(c, ada_w[0], ada_b[0])
    dest, yb_tiles, x1, gates, gt_f = _layer(
        x2d, c_mod, bsz, seq, norm1_w[0], w_in[0], conv_qkv_w[0], a_log[0], dt_bias[0], onorm_w[0],
        w_up_a[0], conv_sc_w[0], w_out_sc[0], w_o[0], norm2_w[0], router_w[0], router_b[0],
        moe_w1[0], moe_b1[0], moe_w2[0], moe_b2[0])
    out = _combine(dest, yb_tiles, x1, gates, gt_f, final_norm_w, seq)
    return out.reshape(bsz, seq, d)
```

```python
import functools

import jax
import jax.numpy as jnp
from jax import lax
from jax.experimental import pallas as pl
from jax.experimental.pallas import tpu as pltpu

F32 = jnp.float32
BF16 = jnp.bfloat16

EPS = 1e-6
N_HEADS = 8
HEAD_DIM = 128
CHUNK = 64
N_EXPERTS = 32
TOP_K = 4
SWIGLU_LIMIT = 7.0
SWIGLU_ALPHA = 1.702
LANES = 128
SUBLANES = 8
MOE_ROWS = 512
SCAN_CHUNKS = 8
VMEM_LIMIT = 56 * 1024 * 1024


def _sigmoid(x):
    return 1.0 / (1.0 + jnp.exp(-x))


def _softplus(x):
    return jnp.maximum(x, 0.0) + jnp.log(1.0 + jnp.exp(-jnp.abs(x)))


def _dot(a, b):
    return jnp.dot(a, b, preferred_element_type=F32)


def _bmm(a, b):
    return lax.dot_general(a, b, (((2,), (1,)), ((0,), (0,))), preferred_element_type=F32)


def _bmm_nt(a, b):
    return lax.dot_general(a, b, (((2,), (2,)), ((0,), (0,))), preferred_element_type=F32)


def _params(*sem, vmem=VMEM_LIMIT):
    return pltpu.CompilerParams(dimension_semantics=sem, vmem_limit_bytes=vmem)


def _rows_to_tiles(ref, value):
    n_rows = value.shape[0]
    for s in range(SUBLANES):
        ref[pl.ds(s, n_rows, stride=SUBLANES), :] = value[:, s * LANES:(s + 1) * LANES]


def _tiles_to_rows(ref, n_rows):
    return jnp.concatenate(
        [ref[pl.ds(s, n_rows, stride=SUBLANES), :] for s in range(SUBLANES)], axis=1)


def _ada_body(c_ref, w_ref, b_ref, o_ref):
    c = c_ref[...]
    ca = c * _sigmoid(c)
    o_ref[...] = _dot(ca.astype(BF16), w_ref[...].astype(BF16)) + b_ref[...]


def _ada_mod(c, ada_w, ada_b):
    bsz, d = c.shape
    n_out = ada_w.shape[1]
    cp = jnp.pad(c, ((0, SUBLANES - bsz), (0, 0)))
    out = pl.pallas_call(
        _ada_body,
        grid=(n_out // d,),
        in_specs=[pl.BlockSpec((SUBLANES, d), lambda j: (0, 0)),
                  pl.BlockSpec((d, d), lambda j: (0, j)),
                  pl.BlockSpec((1, d), lambda j: (0, j))],
        out_specs=pl.BlockSpec((SUBLANES, d), lambda j: (0, j)),
        out_shape=jax.ShapeDtypeStruct((SUBLANES, n_out), F32),
        compiler_params=_params("parallel"),
        name="ada_mod",
    )(cp, ada_w, ada_b.reshape(1, n_out))
    return out[:bsz]


def _inproj_body(x_ref, nw_ref, sc_ref, sh_ref, wa_ref, wb_ref, ws_ref, o_ref, os_ref, h_scr, *,
                 n_a):
    j = pl.program_id(1)

    @pl.when(j == 0)
    def _():
        x = x_ref[...]
        ms = jnp.mean(x * x, axis=-1, keepdims=True)
        xn = x * lax.rsqrt(ms + EPS) * nw_ref[...]
        h = (xn * (1.0 + sc_ref[0]) + sh_ref[0]).astype(BF16)
        h_scr[...] = h
        os_ref[...] = _dot(h, ws_ref[...])

    @pl.when(j < n_a)
    def _():
        o_ref[...] = _dot(h_scr[...], wa_ref[...])

    @pl.when(j >= n_a)
    def _():
        o_ref[...] = _dot(h_scr[...], wb_ref[...])


def _in_projection(x2d, norm_w, sc_m, sh_m, w_a, w_b, w_small, seq):
    n, d = x2d.shape
    tm = min(2048, seq)
    tn = 1024
    n_a = w_a.shape[1] // tn
    width = w_a.shape[1] + w_b.shape[1]
    tiles_per_batch = seq // tm
    return pl.pallas_call(
        functools.partial(_inproj_body, n_a=n_a),
        grid=(n // tm, width // tn),
        in_specs=[pl.BlockSpec((tm, d), lambda i, j: (i, 0)),
                  pl.BlockSpec((1, d), lambda i, j: (0, 0)),
                  pl.BlockSpec((1, 1, d), lambda i, j: (i // tiles_per_batch, 0, 0)),
                  pl.BlockSpec((1, 1, d), lambda i, j: (i // tiles_per_batch, 0, 0)),
                  pl.BlockSpec((d, tn), lambda i, j: (0, jnp.minimum(j, n_a - 1))),
                  pl.BlockSpec((d, tn), lambda i, j: (0, jnp.maximum(j - n_a, 0))),
                  pl.BlockSpec((d, LANES), lambda i, j: (0, 0))],
        out_specs=[pl.BlockSpec((tm, tn), lambda i, j: (i, j)),
                   pl.BlockSpec((tm, LANES), lambda i, j: (i, 0))],
        out_shape=[jax.ShapeDtypeStruct((n, width), F32),
                   jax.ShapeDtypeStruct((n, LANES), F32)],
        scratch_shapes=[pltpu.VMEM((tm, d), BF16)],
        compiler_params=_params("parallel", "arbitrary"),
        name="in_proj",
    )(x2d, norm_w.reshape(1, d), sc_m, sh_m, w_a, w_b, w_small)


def _conv_staged(scr, prev, cur, nxt, cw, width):
    rows = cur.shape[0]
    scr[0:SUBLANES, :] = prev
    scr[SUBLANES:SUBLANES + rows, :] = cur
    scr[SUBLANES + rows:, :] = nxt
    pad = width // 2
    acc = scr[pl.ds(SUBLANES - pad, rows), :] * cw[0:1]
    for w in range(1, width):
        acc = acc + scr[pl.ds(SUBLANES - pad + w, rows), :] * cw[w:w + 1]
    return acc


def _halo_specs(rows, cols, col_block, n_rows):
    per = rows // SUBLANES
    last = n_rows // SUBLANES - 1
    cur = pl.BlockSpec((rows, cols), lambda i, j: (i, col_block(j)))
    prev = pl.BlockSpec((SUBLANES, cols), lambda i, j: (jnp.maximum(i * per - 1, 0), col_block(j)))
    nxt = pl.BlockSpec((SUBLANES, cols), lambda i, j: (jnp.minimum((i + 1) * per, last), col_block(j)))
    return cur, prev, nxt


def _sc_conv_body(b_ref, c_ref, cp_ref, cn_ref, u_ref, up_ref, un_ref, cw_ref, o_ref, *,
                  tiles_per_batch, width):
    t_in_b = pl.program_id(0) % tiles_per_batch
    rows = c_ref.shape[0]
    length = rows + 2 * SUBLANES
    prev = jnp.where(t_in_b == 0, 0.0, cp_ref[...] * up_ref[...])
    nxt = jnp.where(t_in_b == tiles_per_batch - 1, 0.0, cn_ref[...] * un_ref[...])
    ext = jnp.concatenate([prev, c_ref[...] * u_ref[...], nxt], axis=0)
    cw = cw_ref[...]
    pad = width // 2
    acc = ext * cw[pad:pad + 1]
    for w in range(width):
        if w != pad:
            acc = acc + pltpu.roll(ext, (pad - w) % length, axis=0) * cw[w:w + 1]
    o_ref[...] = (b_ref[...] * acc[SUBLANES:SUBLANES + rows]).astype(BF16)


def _sc_conv(p_main, conv_w, seq, first_col_block):
    n = p_main.shape[0]
    width, ch = conv_w.shape
    rows = min(512, seq)
    cw = jnp.pad(conv_w, ((0, SUBLANES - width), (0, 0)))
    b_spec = pl.BlockSpec((rows, ch), lambda i, j: (i, first_col_block))
    c_specs = _halo_specs(rows, ch, lambda j: first_col_block + 1, n)
    u_specs = _halo_specs(rows, ch, lambda j: first_col_block + 2, n)
    body = functools.partial(_sc_conv_body, tiles_per_batch=seq // rows, width=width)
    return pl.pallas_call(
        body,
        grid=(n // rows, 1),
        in_specs=[b_spec, *c_specs, *u_specs, pl.BlockSpec((SUBLANES, ch), lambda i, j: (0, 0))],
        out_specs=pl.BlockSpec((rows, ch), lambda i, j: (i, 0)),
        out_shape=jax.ShapeDtypeStruct((n, ch), BF16),
        compiler_params=_params("parallel", "arbitrary"),
        name="sc_conv",
    )(p_main, p_main, p_main, p_main, p_main, p_main, p_main, cw)


def _block_diag(x, lo_half):
    return jnp.concatenate([jnp.where(lo_half, x, 0.0), jnp.where(lo_half, 0.0, x)], axis=1)


def _delta_local_body(qc_ref, qp_ref, qn_ref, qw_ref, kc_ref, kp_ref, kn_ref, kw_ref,
                      vc_ref, vp_ref, vn_ref, vw_ref, s_ref, alog_ref, dtb_ref,
                      uf_ref, ub_ref, wf_ref, wb_ref, qf_ref, qb_ref, a_ref, kt_ref, ef_ref, eb_ref,
                      beta_scr, g_scr, gl_scr, gt_scr, gtr_scr, cq_scr, ck_scr, cv_scr, *,
                      tiles_per_batch, width):
    h = pl.program_id(1)
    rows = qc_ref.shape[0]
    n_chunks = rows // CHUNK
    lane = lax.broadcasted_iota(jnp.int32, (rows, LANES), 1)

    @pl.when(h == 0)
    def _():
        s = s_ref[...]
        beta_scr[...] = _sigmoid(s)
        g = -jnp.exp(alog_ref[...]) * _softplus(s + dtb_ref[...])
        pos = lax.broadcasted_iota(jnp.int32, (rows, LANES), 0) & (CHUNK - 1)
        gp = g
        gs = g
        step = 1
        while step < CHUNK:
            gp = gp + jnp.where(pos >= step, pltpu.roll(gp, step, axis=0), 0.0)
            gs = gs + jnp.where(pos < CHUNK - step, pltpu.roll(gs, rows - step, axis=0), 0.0)
            step *= 2
        gc = jnp.where(lane >= 3 * N_HEADS, gs, gp)
        g_scr[...] = gc
        gl_scr[...] = gp + gs - g
        for c2 in range(rows // LANES):
            gt = gc[c2 * LANES:(c2 + 1) * LANES, :].T
            gt_scr[c2] = gt
            gtr_scr[c2] = pltpu.roll(gt, CHUNK, axis=1)

    t_in_b = pl.program_id(0) % tiles_per_batch

    def conv_in(cur_ref, prev_ref, next_ref, w_ref, scr, normed):
        prev = jnp.where(t_in_b == 0, 0.0, prev_ref[...])
        nxt = jnp.where(t_in_b == tiles_per_batch - 1, 0.0, next_ref[...])
        y = _conv_staged(scr, prev, cur_ref[...], nxt, w_ref[...], width)
        y = y * _sigmoid(y)
        if normed:
            y = y * lax.rsqrt(jnp.sum(y * y, axis=-1, keepdims=True) + EPS)
        return y

    q2 = conv_in(qc_ref, qp_ref, qn_ref, qw_ref, cq_scr, True) * (HEAD_DIM ** -0.5)
    k2 = conv_in(kc_ref, kp_ref, kn_ref, kw_ref, ck_scr, True)
    v2 = conv_in(vc_ref, vp_ref, vn_ref, vw_ref, cv_scr, False)

    def pick(ref, idx):
        return jnp.sum(jnp.where(lane == idx, ref[...], 0.0), axis=-1, keepdims=True)

    i_f, i_b = 2 * N_HEADS + h, 3 * N_HEADS + h
    beta_f, beta_b = pick(beta_scr, h), pick(beta_scr, N_HEADS + h)
    g_f, g_b = pick(g_scr, i_f), pick(g_scr, i_b)
    gl_f, gl_b = pick(gl_scr, i_f), pick(gl_scr, i_b)

    kb_f, kb_b = k2 * beta_f, k2 * beta_b
    eg_f, eg_b = jnp.exp(g_f), jnp.exp(g_b)
    to3 = lambda a: a.reshape(n_chunks, CHUNK, a.shape[-1])
    bf3 = lambda a: to3(a.astype(BF16))

    k3 = bf3(k2)
    zero3 = jnp.zeros_like(k3)
    lhs = jnp.concatenate([jnp.concatenate([bf3(kb_f), bf3(kb_b)], axis=2),
                           jnp.concatenate([bf3(q2), bf3(q2)], axis=2)], axis=1)
    rhs_t = jnp.concatenate([jnp.concatenate([k3, zero3], axis=2),
                             jnp.concatenate([zero3, k3], axis=2)], axis=1)
    m1 = _bmm_nt(lhs, rhs_t)

    ii = lax.broadcasted_iota(jnp.int32, (CHUNK, LANES), 0)
    l2 = lax.broadcasted_iota(jnp.int32, (CHUNK, LANES), 1)
    jj = l2 & (CHUNK - 1)
    lo_half = l2 < CHUNK
    ahead = jnp.where(lo_half, ii - jj, jj - ii)
    incl = ahead >= 0
    strict = ahead > 0
    eye = (ii == jj).astype(F32)

    g_cols = jnp.where(lane < CHUNK, g_f, g_b)
    lrow = lax.broadcasted_iota(jnp.int32, (1, LANES), 1) < CHUNK
    diffs = []
    for c in range(n_chunks):
        rf = gt_scr[c // 2, pl.ds(i_f, 1), :]
        rb = gt_scr[c // 2, pl.ds(i_b, 1), :]
        if c % 2 == 0:
            g_row = jnp.where(lrow, rf, gtr_scr[c // 2, pl.ds(i_b, 1), :])
        else:
            g_row = jnp.where(lrow, gtr_scr[c // 2, pl.ds(i_f, 1), :], rb)
        diffs.append(g_cols[c * CHUNK:(c + 1) * CHUNK] - g_row)
    decay = jnp.exp(jnp.where(incl, jnp.stack(diffs), -jnp.inf))

    nm = -jnp.where(strict, m1[:, :CHUNK] * decay, 0.0)
    a_ref[...] = (m1[:, CHUNK:] * decay).astype(BF16).reshape(rows, LANES)

    p = eye + nm
    npow = _bmm(nm.astype(BF16), _block_diag(nm, lo_half).astype(BF16))
    n_iter = CHUNK.bit_length() - 2
    for it in range(n_iter):
        bd = _block_diag(npow, lo_half).astype(BF16)
        if it + 1 < n_iter:
            x = _bmm(jnp.concatenate([p, npow], axis=1).astype(BF16), bd)
            p = p + x[:, :CHUNK]
            npow = x[:, CHUNK:]
        else:
            p = p + _bmm(p.astype(BF16), bd)

    rhs = jnp.concatenate([jnp.concatenate([bf3(v2 * beta_f), bf3(kb_f * eg_f)], axis=2),
                           jnp.concatenate([bf3(v2 * beta_b), bf3(kb_b * eg_b)], axis=2)], axis=1)
    uw = _bmm(_block_diag(p, lo_half).astype(BF16), rhs)
    uf_ref[...] = uw[:, :CHUNK, :HEAD_DIM].reshape(rows, HEAD_DIM)
    ub_ref[...] = uw[:, CHUNK:, :HEAD_DIM].reshape(rows, HEAD_DIM)
    wf_ref[...] = uw[:, :CHUNK, HEAD_DIM:].astype(BF16).reshape(rows, HEAD_DIM)
    wb_ref[...] = uw[:, CHUNK:, HEAD_DIM:].astype(BF16).reshape(rows, HEAD_DIM)
    qf_ref[...] = (q2 * eg_f).astype(BF16)
    qb_ref[...] = (q2 * eg_b).astype(BF16)

    kd_f = to3(k2 * jnp.exp(gl_f - g_f))
    kd_b = to3(k2 * jnp.exp(gl_b - g_b))
    e_f = to3(jnp.broadcast_to(jnp.exp(gl_f), (rows, LANES)))
    e_b = to3(jnp.broadcast_to(jnp.exp(gl_b), (rows, LANES)))
    ef_ref[...] = e_f[:, :SUBLANES, :].reshape(n_chunks * SUBLANES, LANES)
    eb_ref[...] = e_b[:, :SUBLANES, :].reshape(n_chunks * SUBLANES, LANES)
    for c in range(n_chunks):
        kt_ref[c * 2 * CHUNK:(c + 1) * 2 * CHUNK, :] = (
            jnp.concatenate([kd_f[c], kd_b[c]], axis=0).T.astype(BF16))


def _delta_local(p_main, p_small, conv_w, a_log, dt_bias, seq):
    n = p_main.shape[0]
    width = conv_w.shape[0]
    cols = N_HEADS * HEAD_DIM
    rows = min(2048, seq)
    cw = jnp.pad(conv_w, ((0, SUBLANES - width), (0, 0)))
    alog_row = jnp.pad(a_log.reshape(1, -1), ((0, 0), (2 * N_HEADS, LANES - 4 * N_HEADS)))
    dtb_row = jnp.pad(dt_bias.reshape(1, -1), ((0, 0), (2 * N_HEADS, LANES - 4 * N_HEADS)))

    def qkv_specs(off):
        col = lambda h: off + h
        return [*_halo_specs(rows, HEAD_DIM, col, n),
                pl.BlockSpec((SUBLANES, HEAD_DIM), lambda i, h: (0, off + h))]

    row_spec = pl.BlockSpec((1, LANES), lambda i, h: (0, 0))
    out_blk = pl.BlockSpec((rows, HEAD_DIM), lambda i, h: (i, h))
    f32_out = jax.ShapeDtypeStruct((n, cols), F32)
    bf_out = jax.ShapeDtypeStruct((n, cols), BF16)
    eg_out = jax.ShapeDtypeStruct((n // SUBLANES, cols), F32)
    eg_blk = pl.BlockSpec((rows // SUBLANES, HEAD_DIM), lambda i, h: (i, h))
    tile = pltpu.VMEM((rows, LANES), F32)
    tposed = pltpu.VMEM((rows // LANES, LANES, LANES), F32)
    staged = pltpu.VMEM((rows + 2 * SUBLANES, HEAD_DIM), F32)
    body = functools.partial(_delta_local_body, tiles_per_batch=seq // rows, width=width)
    qkv_in = [p_main, p_main, p_main, cw]
    return pl.pallas_call(
        body,
        grid=(n // rows, N_HEADS),
        in_specs=[*qkv_specs(0), *qkv_specs(N_HEADS), *qkv_specs(2 * N_HEADS),
                  pl.BlockSpec((rows, LANES), lambda i, h: (i, 0)), row_spec, row_spec],
        out_specs=[out_blk, out_blk, out_blk, out_blk, out_blk, out_blk, out_blk,
                   pl.BlockSpec((2 * rows, HEAD_DIM), lambda i, h: (i, h)), eg_blk, eg_blk],
        out_shape=[f32_out, f32_out, bf_out, bf_out, bf_out, bf_out, bf_out,
                   jax.ShapeDtypeStruct((2 * n, cols), BF16), eg_out, eg_out],
        scratch_shapes=[tile, tile, tile, tposed, tposed, staged, staged, staged],
        compiler_params=_params("parallel", "arbitrary"),
        name="delta_local",
    )(*qkv_in, *qkv_in, *qkv_in, p_small, alog_row, dtb_row)


def _delta_scan_body(uf_ref, wf_ref, qf_ref, af_ref, ktf_ref, ef_ref,
                     ub_ref, wb_ref, qb_ref, ab_ref, ktb_ref, eb_ref,
                     of_ref, ob_ref, s_scr, *, sub_chunks):
    @pl.when(pl.program_id(1) == 0)
    def _():
        s_scr[...] = jnp.zeros_like(s_scr)

    heads = [slice(h * HEAD_DIM, (h + 1) * HEAD_DIM) for h in range(N_HEADS)]
    state = s_scr[...]
    for j in range(sub_chunks):
        jb = sub_chunks - 1 - j

        def per_head(ref_f, ref_b, unit, rows=None, half=False):
            rows = unit if rows is None else rows
            lo, hi = ((slice(0, CHUNK), slice(CHUNK, 2 * CHUNK)) if half
                      else (slice(None), slice(None)))
            return jnp.stack([ref_f[j * unit:j * unit + rows, hs][:, lo] for hs in heads]
                             + [ref_b[jb * unit:jb * unit + rows, hs][:, hi] for hs in heads])

        wq = jnp.concatenate([per_head(wf_ref, wb_ref, CHUNK), per_head(qf_ref, qb_ref, CHUNK)],
                             axis=1)
        x = _bmm(wq, state.astype(BF16))
        v_new = (per_head(uf_ref, ub_ref, CHUNK) - x[:, :CHUNK]).astype(BF16)
        lhs = jnp.concatenate([per_head(af_ref, ab_ref, CHUNK, half=True),
                               per_head(ktf_ref, ktb_ref, 2 * CHUNK, half=True)], axis=1)
        y = _bmm(lhs, v_new)
        out = x[:, CHUNK:] + y[:, :CHUNK]
        for h, hs in enumerate(heads):
            of_ref[j * CHUNK:(j + 1) * CHUNK, hs] = out[h]
            ob_ref[jb * CHUNK:(jb + 1) * CHUNK, hs] = out[N_HEADS + h]
        state = state * per_head(ef_ref, eb_ref, SUBLANES, rows=1) + y[:, CHUNK:]
    s_scr[...] = state


def _delta_scan(local_out, bsz, seq):
    uf, ub, wf, wb, qf, qb, a, kt, ef, eb = local_out
    n, cols = uf.shape
    n_chunks = seq // CHUNK
    sub = min(SCAN_CHUNKS, n_chunks)
    steps = n_chunks // sub
    fwd = lambda b, c: (b * steps + c, 0)
    bwd = lambda b, c: (b * steps + (steps - 1 - c), 0)
    blk = lambda unit, idx: pl.BlockSpec((sub * unit, cols), idx)
    specs = lambda idx: [blk(CHUNK, idx), blk(CHUNK, idx), blk(CHUNK, idx), blk(CHUNK, idx),
                         blk(2 * CHUNK, idx), blk(SUBLANES, idx)]
    return pl.pallas_call(
        functools.partial(_delta_scan_body, sub_chunks=sub),
        grid=(bsz, steps),
        in_specs=specs(fwd) + specs(bwd),
        out_specs=[blk(CHUNK, fwd), blk(CHUNK, bwd)],
        out_shape=[jax.ShapeDtypeStruct((n, cols), F32), jax.ShapeDtypeStruct((n, cols), F32)],
        scratch_shapes=[pltpu.VMEM((2 * N_HEADS, HEAD_DIM, HEAD_DIM), F32)],
        compiler_params=_params("parallel", "arbitrary"),
        name="delta_scan",
    )(uf, wf, qf, a, kt, ef, ub, wb, qb, a, kt, eb)


def _mixer_out_body(of_ref, ob_ref, z_ref, yb_ref, ga_ref, gb_ref, x_ref,
                    onw_ref, wa_ref, wb_ref, wo_ref, gtm_ref, n2w_ref, scf_ref, shf_ref,
                    rwh_ref, rwl_ref, rb_ref, x1_ref, h2_ref, lg_ref, oz_scr):
    o = of_ref[...] + ob_ref[...]
    z = z_ref[...]
    onw = onw_ref[...]
    for h in range(N_HEADS):
        hs = slice(h * HEAD_DIM, (h + 1) * HEAD_DIM)
        oh = o[:, hs]
        ms = jnp.mean(oh * oh, axis=-1, keepdims=True)
        zh = z[:, hs]
        oz_scr[:, hs] = (oh * lax.rsqrt(ms + EPS) * onw * (zh * _sigmoid(zh))).astype(BF16)
    y_a = _dot(oz_scr[...], wa_ref[...])
    y_b = _dot(yb_ref[...], wb_ref[...])
    merged = _sigmoid(ga_ref[...]) * y_a + _sigmoid(gb_ref[...]) * y_b
    x1 = x_ref[...] + gtm_ref[0] * _dot(merged.astype(BF16), wo_ref[...])
    x1_ref[...] = x1
    ms = jnp.mean(x1 * x1, axis=-1, keepdims=True)
    h2 = x1 * lax.rsqrt(ms + EPS) * n2w_ref[...] * (1.0 + scf_ref[0]) + shf_ref[0]
    _rows_to_tiles(h2_ref, h2)
    h_hi = h2.astype(BF16)
    h_lo = (h2 - h_hi.astype(F32)).astype(BF16)
    rwh = rwh_ref[...]
    lg_ref[...] = (_dot(h_hi, rwh) + (_dot(h_lo, rwh) + _dot(h_hi, rwl_ref[...]))) + rb_ref[...]


def _mixer_out(o_f, o_b, p_main, y_sc, x2d, onorm_w, w_up_a, w_out_sc, w_o, gt_m, norm2_w,
               sc_f, sh_f, router_w, router_b, seq, z_blk, ga_blk):
    n, d = x2d.shape
    tm = min(512, seq)
    tiles_per_batch = seq // tm
    row = lambda i: (i, 0)
    const = lambda i: (0, 0)
    per_b = lambda i: (i // tiles_per_batch, 0, 0)
    weight = pl.BlockSpec((d, d), const, pipeline_mode=pl.Buffered(1))
    rw = jnp.pad(router_w, ((0, 0), (0, LANES - N_EXPERTS)))
    rw_hi = rw.astype(BF16)
    rw_lo = (rw - rw_hi.astype(F32)).astype(BF16)
    rb = jnp.pad(router_b.reshape(1, -1), ((0, 0), (0, LANES - N_EXPERTS)))
    return pl.pallas_call(
        _mixer_out_body,
        grid=(n // tm,),
        in_specs=[pl.BlockSpec((tm, d), row), pl.BlockSpec((tm, d), row),
                  pl.BlockSpec((tm, d), lambda i: (i, z_blk)), pl.BlockSpec((tm, d), row),
                  pl.BlockSpec((tm, d), lambda i: (i, ga_blk)),
                  pl.BlockSpec((tm, d), lambda i: (i, ga_blk + 1)), pl.BlockSpec((tm, d), row),
                  pl.BlockSpec((1, HEAD_DIM), const), weight, weight, weight,
                  pl.BlockSpec((1, 1, d), per_b), pl.BlockSpec((1, d), const),
                  pl.BlockSpec((1, 1, d), per_b), pl.BlockSpec((1, 1, d), per_b),
                  pl.BlockSpec((d, LANES), const), pl.BlockSpec((d, LANES), const),
                  pl.BlockSpec((1, LANES), const)],
        out_specs=[pl.BlockSpec((tm, d), row), pl.BlockSpec((tm * SUBLANES, LANES), row),
                   pl.BlockSpec((tm, LANES), row)],
        out_shape=[jax.ShapeDtypeStruct((n, d), F32),
                   jax.ShapeDtypeStruct((n * SUBLANES, LANES), F32),
                   jax.ShapeDtypeStruct((n, LANES), F32)],
        scratch_shapes=[pltpu.VMEM((tm, d), BF16)],
        compiler_params=_params("parallel"),
        name="mixer_out",
    )(o_f, o_b, p_main, y_sc, p_main, p_main, x2d, onorm_w.reshape(1, HEAD_DIM),
      w_up_a.astype(BF16), w_out_sc.astype(BF16), w_o.astype(BF16), gt_m,
      norm2_w.reshape(1, d), sc_f, sh_f, rw_hi, rw_lo, rb)


def _route_body(lg_ref, er_ref, gate_ref, cnt_ref, carry_scr):
    @pl.when(pl.program_id(0) == 0)
    def _():
        carry_scr[...] = jnp.zeros_like(carry_scr)

    rows = lg_ref.shape[0]
    lane = lax.broadcasted_iota(jnp.int32, (rows, LANES), 1)
    lanef = lane.astype(F32)
    logits = jnp.where(lane < N_EXPERTS, lg_ref[...], -jnp.inf)
    vals, idxs = [], []
    sel = jnp.zeros((rows, LANES), F32)
    for _ in range(TOP_K):
        m = jnp.max(logits, axis=-1, keepdims=True)
        idx = jnp.min(jnp.where(logits == m, lanef, float(LANES)), axis=-1, keepdims=True)
        hit = lanef == idx
        vals.append(m)
        idxs.append(idx)
        sel = sel + hit.astype(F32)
        logits = jnp.where(hit, -jnp.inf, logits)
    exps = [jnp.exp(v - vals[0]) for v in vals]
    total = exps[0]
    for e in exps[1:]:
        total = total + e
    r = lax.broadcasted_iota(jnp.int32, (rows, rows), 0)
    c = lax.broadcasted_iota(jnp.int32, (rows, rows), 1)
    earlier = (r > c).astype(BF16)
    carry = carry_scr[0:1, :]
    before = _dot(earlier, sel.astype(BF16)) + carry
    er = jnp.zeros((rows, LANES), F32)
    gates = jnp.zeros((rows, LANES), F32)
    for k in range(TOP_K):
        rank = jnp.sum(jnp.where(lanef == idxs[k], before, 0.0), axis=-1, keepdims=True)
        er = jnp.where(lane == k, idxs[k], er)
        er = jnp.where(lane == TOP_K + k, rank, er)
        gates = jnp.where(lane == k, exps[k] / total, gates)
    er_ref[...] = er
    gate_ref[...] = gates
    new_carry = carry + jnp.sum(sel, axis=0, keepdims=True)
    carry_scr[...] = jnp.broadcast_to(new_carry, carry_scr.shape)
    cnt_ref[...] = jnp.broadcast_to(new_carry, cnt_ref.shape)


def _route(logits):
    n = logits.shape[0]
    rows = min(512, n)
    tile = pl.BlockSpec((rows, LANES), lambda i: (i, 0))
    return pl.pallas_call(
        _route_body,
        grid=(n // rows,),
        in_specs=[tile],
        out_specs=[tile, tile, pl.BlockSpec((SUBLANES, LANES), lambda i: (0, 0))],
        out_shape=[jax.ShapeDtypeStruct((n, LANES), F32), jax.ShapeDtypeStruct((n, LANES), F32),
                   jax.ShapeDtypeStruct((SUBLANES, LANES), F32)],
        scratch_shapes=[pltpu.VMEM((SUBLANES, LANES), F32)],
        compiler_params=_params("arbitrary"),
        name="route",
    )(logits)


def _dest_body(er_ref, ps_ref, o_ref):
    rows = er_ref.shape[0]
    lane = lax.broadcasted_iota(jnp.int32, (rows, LANES), 1)
    lanef = lane.astype(F32)
    er = er_ref[...]
    starts = ps_ref[0:1, :]
    out = jnp.zeros((rows, LANES), F32)
    for k in range(TOP_K):
        e_k = jnp.sum(jnp.where(lane == k, er, 0.0), axis=-1, keepdims=True)
        r_k = jnp.sum(jnp.where(lane == TOP_K + k, er, 0.0), axis=-1, keepdims=True)
        base = jnp.sum(jnp.where(lanef == e_k, starts, 0.0), axis=-1, keepdims=True)
        out = jnp.where(lane == k, base + r_k, out)
    o_ref[...] = out.astype(jnp.int32)


def _dest_rows(er, pad_start_row):
    n = er.shape[0]
    rows = min(1024, n)
    tile = pl.BlockSpec((rows, LANES), lambda i: (i, 0))
    return pl.pallas_call(
        _dest_body,
        grid=(n // rows,),
        in_specs=[tile, pl.BlockSpec((SUBLANES, LANES), lambda i: (0, 0))],
        out_specs=tile,
        out_shape=jax.ShapeDtypeStruct((n, LANES), jnp.int32),
        compiler_params=_params("parallel"),
        name="dest_rows",
    )(er, pad_start_row)


def _tile_at(ref, row):
    return ref.at[pl.ds(pl.multiple_of(row * SUBLANES, SUBLANES), SUBLANES)]


def _dispatch_body(zero_ref, dest_ref, h_ref, xs_hbm, zbuf, sem, zsem):
    n_tok = dest_ref.shape[-1] // TOP_K
    blk_tiles = zbuf.shape[0]

    @pl.when(pl.program_id(0) == 0)
    def _():
        zbuf[...] = jnp.zeros_like(zbuf)

        def block_copy(b):
            return pltpu.make_async_copy(
                zbuf, xs_hbm.at[pl.ds(pl.multiple_of(b * blk_tiles, blk_tiles), blk_tiles)], zsem)

        def fill(b, carry):
            @pl.when(zero_ref[b] != 0)
            def _():
                block_copy(b).start()
            return carry

        def fill_done(b, carry):
            @pl.when(zero_ref[b] != 0)
            def _():
                block_copy(b).wait()
            return carry

        lax.fori_loop(0, zero_ref.shape[0], fill, 0)
        lax.fori_loop(0, zero_ref.shape[0], fill_done, 0)

    def issue(t, carry):
        for k in range(TOP_K):
            pltpu.make_async_copy(_tile_at(h_ref, t), _tile_at(xs_hbm, dest_ref[0, 0, t * TOP_K + k]),
                                  sem).start(priority=k % 2)
        return carry

    lax.fori_loop(0, n_tok, issue, 0)
    for _ in range(TOP_K):
        pltpu.make_async_copy(h_ref, xs_hbm.at[pl.ds(0, h_ref.shape[0])], sem).wait()


def _dispatch(dest_flat, zero_blocks, h2_tiles, n_tok, m_pad):
    rows = min(512, n_tok)
    dest3 = dest_flat.reshape(n_tok // rows, 1, rows * TOP_K)
    grid_spec = pltpu.PrefetchScalarGridSpec(
        num_scalar_prefetch=1,
        grid=(n_tok // rows,),
        in_specs=[pl.BlockSpec((1, 1, rows * TOP_K), lambda i, zb: (i, 0, 0),
                               memory_space=pltpu.SMEM),
                  pl.BlockSpec((rows * SUBLANES, LANES), lambda i, zb: (i, 0))],
        out_specs=pl.BlockSpec(memory_space=pl.ANY),
        scratch_shapes=[pltpu.VMEM((MOE_ROWS * SUBLANES, LANES), h2_tiles.dtype),
                        pltpu.SemaphoreType.DMA, pltpu.SemaphoreType.DMA],
    )
    return pl.pallas_call(
        _dispatch_body,
        grid_spec=grid_spec,
        out_shape=jax.ShapeDtypeStruct((m_pad * SUBLANES, LANES), h2_tiles.dtype),
        compiler_params=_params("arbitrary"),
        name="dispatch",
    )(zero_blocks, dest3, h2_tiles)


def _expert_body(be_ref, nu_ref, x_ref, w1_ref, b1_ref, w2_ref, b2_ref, o_ref):
    del be_ref
    i = pl.program_id(0)

    @pl.when(i < nu_ref[0])
    def _():
        d_ff = w2_ref.shape[1]
        x = _tiles_to_rows(x_ref, MOE_ROWS)
        gu = _dot(x.astype(BF16), w1_ref[0].astype(BF16)) + b1_ref[0]
        gate = jnp.minimum(gu[:, :d_ff], SWIGLU_LIMIT)
        up = jnp.clip(gu[:, d_ff:], -SWIGLU_LIMIT, SWIGLU_LIMIT)
        act = gate * _sigmoid(SWIGLU_ALPHA * gate) * (up + 1.0)
        _rows_to_tiles(o_ref, _dot(act.astype(BF16), w2_ref[0].astype(BF16)) + b2_ref[0])

    @pl.when(i >= nu_ref[0])
    def _():
        o_ref[...] = jnp.zeros_like(o_ref)


def _experts(xs_tiles, blk_exp, n_used, w1, b1, w2, b2):
    n_exp, d, two_ff = w1.shape
    d_ff = two_ff // 2
    m_pad = xs_tiles.shape[0] // SUBLANES
    out_tiles = pl.BlockSpec((MOE_ROWS * SUBLANES, LANES), lambda i, be, nu: (i, 0))
    in_tiles = pl.BlockSpec((MOE_ROWS * SUBLANES, LANES),
                            lambda i, be, nu: (jnp.minimum(i, nu[0] - 1), 0))
    by_expert = lambda i, be, nu: (be[i], 0, 0)
    grid_spec = pltpu.PrefetchScalarGridSpec(
        num_scalar_prefetch=2,
        grid=(m_pad // MOE_ROWS,),
        in_specs=[in_tiles,
                  pl.BlockSpec((1, d, two_ff), by_expert), pl.BlockSpec((1, 1, two_ff), by_expert),
                  pl.BlockSpec((1, d_ff, d), by_expert), pl.BlockSpec((1, 1, d), by_expert)],
        out_specs=out_tiles,
    )
    return pl.pallas_call(
        _expert_body,
        grid_spec=grid_spec,
        out_shape=jax.ShapeDtypeStruct((m_pad * SUBLANES, LANES), F32),
        compiler_params=_params("arbitrary", vmem=60 * 1024 * 1024),
        name="experts",
    )(blk_exp, n_used, xs_tiles, w1, b1.reshape(n_exp, 1, two_ff), w2, b2.reshape(n_exp, 1, d))


def _combine_body(dcur_ref, dnext_ref, y_hbm, x1_ref, gate_ref, gtf_ref, fw_ref, o_ref, buf, sems):
    step = pl.program_id(0)
    half = x1_ref.shape[0] // 2

    def row_copy(d_ref, tile, slot, t, k):
        return pltpu.make_async_copy(
            _tile_at(y_hbm, d_ref[0, 0, (tile * half + t) * TOP_K + k]),
            _tile_at(buf.at[slot, k], t), sems.at[slot])

    def gather(d_ref, tile, slot):
        def issue(t, carry):
            for k in range(TOP_K):
                row_copy(d_ref, tile, slot, t, k).start(priority=k % 2)
            return carry
        lax.fori_loop(0, half, issue, 0)

    def finish(slot, rows):
        for k in range(TOP_K):
            pltpu.make_async_copy(y_hbm.at[pl.ds(0, half * SUBLANES)], buf.at[slot, k],
                                  sems.at[slot]).wait()
        gates = gate_ref[rows, :]
        moe = gates[:, 0:1] * _tiles_to_rows(buf.at[slot, 0], half)
        for k in range(1, TOP_K):
            moe = moe + gates[:, k:k + 1] * _tiles_to_rows(buf.at[slot, k], half)
        x2 = x1_ref[rows, :] + gtf_ref[0] * moe
        ms = jnp.mean(x2 * x2, axis=-1, keepdims=True)
        o_ref[rows, :] = x2 * lax.rsqrt(ms + EPS) * fw_ref[...]

    @pl.when(step == 0)
    def _():
        gather(dcur_ref, 0, 0)

    gather(dcur_ref, 1, 1)
    finish(0, slice(0, half))

    @pl.when(step + 1 < pl.num_programs(0))
    def _():
        gather(dnext_ref, 0, 0)

    finish(1, slice(half, 2 * half))


def _combine(dest_flat, yb_tiles, x1, gates, gt_f, final_w, seq):
    n, d = x1.shape
    rows = min(512, seq)
    half = rows // 2
    steps = n // rows
    tiles_per_batch = seq // rows
    dest3 = dest_flat.reshape(steps, 1, rows * TOP_K)
    dest_blk = lambda idx: pl.BlockSpec((1, 1, rows * TOP_K), idx, memory_space=pltpu.SMEM)
    return pl.pallas_call(
        _combine_body,
        grid=(steps,),
        in_specs=[dest_blk(lambda i: (i, 0, 0)),
                  dest_blk(lambda i: (jnp.minimum(i + 1, steps - 1), 0, 0)),
                  pl.BlockSpec(memory_space=pl.ANY),
                  pl.BlockSpec((rows, d), lambda i: (i, 0)),
                  pl.BlockSpec((rows, LANES), lambda i: (i, 0)),
                  pl.BlockSpec((1, 1, d), lambda i: (i // tiles_per_batch, 0, 0)),
                  pl.BlockSpec((1, d), lambda i: (0, 0))],
        out_specs=pl.BlockSpec((rows, d), lambda i: (i, 0)),
        out_shape=jax.ShapeDtypeStruct((n, d), F32),
        scratch_shapes=[pltpu.VMEM((2, TOP_K, half * SUBLANES, LANES), F32),
                        pltpu.SemaphoreType.DMA((2,))],
        compiler_params=_params("arbitrary"),
        name="combine",
    )(dest3, dest3, yb_tiles, x1, gates, gt_f, final_w.reshape(1, d))


def _layer(x2d, c_mod, bsz, seq, norm1_w, w_in, conv_qkv_w, a_log, dt_bias, onorm_w, w_up_a,
           conv_sc_w, w_out_sc, w_o, norm2_w, router_w, router_b, moe_w1, moe_b1, moe_w2, moe_b2):
    n, d = x2d.shape
    dn = N_HEADS * HEAD_DIM
    sh_m, sc_m, gt_m, sh_f, sc_f, gt_f = (m.reshape(bsz, 1, d) for m in jnp.split(c_mod, 6, axis=-1))

    o_small, o_sc = 4 * dn, 4 * dn + 4 * N_HEADS
    w_a = w_in[:, :o_small].astype(BF16)
    w_b = w_in[:, o_sc:].astype(BF16)
    w_small = jnp.pad(w_in[:, o_small:o_sc], ((0, 0), (0, LANES - 4 * N_HEADS))).astype(BF16)
    p_main, p_small = _in_projection(x2d, norm1_w, sc_m, sh_m, w_a, w_b, w_small, seq)

    y_sc = _sc_conv(p_main, conv_sc_w, seq, first_col_block=4 * dn // d)
    local = _delta_local(p_main, p_small, conv_qkv_w, a_log, dt_bias, seq)
    o_f, o_b = _delta_scan(local, bsz, seq)
    x1, h2_tiles, logits = _mixer_out(o_f, o_b, p_main, y_sc, x2d, onorm_w, w_up_a, w_out_sc, w_o,
                                      gt_m, norm2_w, sc_f, sh_f, router_w, router_b, seq,
                                      z_blk=3 * dn // d, ga_blk=(4 * dn + 3 * d) // d)

    er, gates, counts = _route(logits)
    m_rows = n * TOP_K
    n_blocks = m_rows // MOE_ROWS + N_EXPERTS
    cnt = counts[0, :N_EXPERTS].astype(jnp.int32)
    padded = (cnt + MOE_ROWS - 1) // MOE_ROWS * MOE_ROWS
    pad_end = jnp.cumsum(padded)
    pad_start = pad_end - padded
    blk_row0 = jnp.arange(n_blocks, dtype=jnp.int32) * MOE_ROWS
    blk_exp = jnp.minimum(jnp.sum(pad_end[None, :] <= blk_row0[:, None], axis=1),
                          N_EXPERTS - 1).astype(jnp.int32)
    n_used = (pad_end[-1:] // MOE_ROWS).astype(jnp.int32)
    ends_padded = jnp.any((pad_end[None, :] == (blk_row0 + MOE_ROWS)[:, None])
                          & (padded != cnt)[None, :], axis=1)
    zero_blocks = (ends_padded | (blk_row0 >= pad_end[-1])).astype(jnp.int32)
    ps_row = jnp.broadcast_to(
        jnp.pad(pad_start.astype(F32), (0, LANES - N_EXPERTS)).reshape(1, LANES), (SUBLANES, LANES))
    dest = _dest_rows(er, ps_row)[:, :TOP_K].reshape(m_rows)

    xs_tiles = _dispatch(dest, zero_blocks, h2_tiles, n, n_blocks * MOE_ROWS)
    yb_tiles = _experts(xs_tiles, blk_exp, n_used, moe_w1, moe_b1, moe_w2, moe_b2)
    return dest, yb_tiles, x1, gates, gt_f


def kernel(x, c, ada_w, ada_b, norm1_w, w_in, conv_qkv_w, a_log, dt_bias, onorm_w, w_up_a, conv_sc_w, w_out_sc, w_o, norm2_w, router_w, router_b, moe_w1, moe_b1, moe_w2, moe_b2, final_norm_w):
    bsz, seq, d = x.shape
    depth = ada_w.shape[0]
    assert depth == 1, "the combine stage fuses the final norm, which needs a single layer"
    x2d = x.reshape(bsz * seq, d)
    c_mod = _ada_mod(c, ada_w[0], ada_b[0])
    dest, yb_tiles, x1, gates, gt_f = _layer(
        x2d, c_mod, bsz, seq, norm1_w[0], w_in[0], conv_qkv_w[0], a_log[0], dt_bias[0], onorm_w[0],
        w_up_a[0], conv_sc_w[0], w_out_sc[0], w_o[0], norm2_w[0], router_w[0], router_b[0],
        moe_w1[0], moe_b1[0], moe_w2[0], moe_b2[0])
    out = _combine(dest, yb_tiles, x1, gates, gt_f, final_norm_w, seq)
    return out.reshape(bsz, seq, d)
```
